```python
import jax, jax.numpy as jnp
from jax import lax
import numpy as np

D_MODEL = 1024
BATCH = 2
SEQ = 8192
DEPTH = 4

GRID_W = 64
CTX_LEN = 256
N_MIXERS = 2
N_ATTN_LAYERS = (DEPTH + N_MIXERS - 1) // N_MIXERS
N_FOURIER_LAYERS = DEPTH // N_MIXERS

DA_HEAD_DIM = 64
DA_HEADS = D_MODEL // (2 * DA_HEAD_DIM)
DA_V_DIM = 2 * DA_HEAD_DIM
DA_QK_WIDTH = DA_HEADS * 2 * DA_HEAD_DIM
DA_V_WIDTH = DA_HEADS * DA_V_DIM
Q_BLOCK = 128
ROPE_THETA = 10000.0
ROPE_FREQS = DA_HEAD_DIM // 4

FN_WIDTH = D_MODEL
FN_GROUPS = 8
FN_GROUP_DIM = FN_WIDTH // FN_GROUPS

N_EXPERTS = 32
TOP_K = 4
D_EXPERT = D_MODEL
SWIGLU_LIMIT = 7.0
SWIGLU_ALPHA = 1.702
MOE_BLOCK = 256

LN_EPS = 1e-5
RMS_EPS = 1e-5
DEEPNORM_ALPHA = (2 * DEPTH) ** 0.25
DEEPNORM_BETA = (8 * DEPTH) ** -0.25

kernel_name = "hybrid_diffattn_fnet_moe_dit"


def layer_norm(x, g, b):
    xf = x.astype(jnp.float32)
    mu = jnp.mean(xf, axis=-1, keepdims=True)
    var = jnp.mean(jnp.square(xf - mu), axis=-1, keepdims=True)
    return ((xf - mu) * lax.rsqrt(var + LN_EPS)).astype(x.dtype) * g + b


def rms_norm(x, g):
    xf = x.astype(jnp.float32)
    return (xf * lax.rsqrt(jnp.mean(xf * xf, axis=-1, keepdims=True) + RMS_EPS)).astype(x.dtype) * g


def axial_rope_tables(n_tokens, dtype):
    rows = n_tokens // GRID_W
    row_ids = jnp.repeat(jnp.arange(rows), GRID_W).astype(jnp.float32)
    col_ids = jnp.tile(jnp.arange(GRID_W), rows).astype(jnp.float32)
    inv_freq = ROPE_THETA ** (-jnp.arange(ROPE_FREQS, dtype=jnp.float32) / ROPE_FREQS)
    ang = jnp.stack([row_ids[:, None] * inv_freq, col_ids[:, None] * inv_freq], axis=1)
    return jnp.cos(ang).astype(dtype), jnp.sin(ang).astype(dtype)


def apply_axial_rope(x, cos, sin):
    xr = x.reshape(x.shape[:-1] + (2, 2, ROPE_FREQS))
    x1, x2 = xr[..., 0, :], xr[..., 1, :]
    c_ = cos[None, :, None, None]
    s_ = sin[None, :, None, None]
    out = jnp.stack([x1 * c_ - x2 * s_, x2 * c_ + x1 * s_], axis=-2)
    return out.reshape(x.shape)


def diff_attend(q, k, v, lam):
    s = jnp.einsum("bqhcd,bkhcd->bhcqk", q, k).astype(jnp.float32) * (DA_HEAD_DIM ** -0.5)
    p = jax.nn.softmax(s, axis=-1)
    a = p[:, :, 0] - lam * p[:, :, 1]
    return jnp.einsum("bhqk,bkhe->bqhe", a.astype(v.dtype), v)


def diff_attention(h, h_ctx, w_qkv, lam, subln_g, w_o, cos, sin, lambda_init, ctx_queries):
    B, N, _ = h.shape
    Lc = h_ctx.shape[1]
    q, k, v = jnp.split(h @ w_qkv, [DA_QK_WIDTH, 2 * DA_QK_WIDTH], axis=-1)
    q = apply_axial_rope(q.reshape(B, N, DA_HEADS, 2, DA_HEAD_DIM), cos, sin)
    k = apply_axial_rope(k.reshape(B, N, DA_HEADS, 2, DA_HEAD_DIM), cos, sin)
    v = v.reshape(B, N, DA_HEADS, DA_V_DIM)
    k_c, v_c = jnp.split(h_ctx @ w_qkv[:, DA_QK_WIDTH:], [DA_QK_WIDTH], axis=-1)
    k_c = k_c.reshape(B, Lc, DA_HEADS, 2, DA_HEAD_DIM)
    v_c = v_c.reshape(B, Lc, DA_HEADS, DA_V_DIM)

    lf = lam.astype(jnp.float32)
    lam_val = jnp.exp(jnp.sum(lf[0] * lf[1])) - jnp.exp(jnp.sum(lf[2] * lf[3])) + lambda_init

    def finish(o):
        L = o.shape[1]
        o = rms_norm(o, subln_g) * (1.0 - lambda_init)
        return o.reshape(B, L, DA_V_WIDTH) @ w_o

    k_all = jnp.concatenate([k, k_c], axis=1)
    v_all = jnp.concatenate([v, v_c], axis=1)
    nb = N // Q_BLOCK
    q_blocks = jnp.moveaxis(q.reshape(B, nb, Q_BLOCK, DA_HEADS, 2, DA_HEAD_DIM), 1, 0)
    o = lax.map(lambda qb: diff_attend(qb, k_all, v_all, lam_val), q_blocks)
    o = jnp.moveaxis(o, 0, 1).reshape(B, N, DA_HEADS, DA_V_DIM)
    y = finish(o)

    if ctx_queries:
        q_c = (h_ctx @ w_qkv[:, :DA_QK_WIDTH]).reshape(B, Lc, DA_HEADS, 2, DA_HEAD_DIM)
        y_c = finish(diff_attend(q_c, k_c, v_c, lam_val))
    else:
        y_c = None
    return y, y_c


def fourier_mix(h, w_in, norm_g, w_out):
    B, L, _ = h.shape
    u = (h @ w_in).reshape(B, L, FN_GROUPS, FN_GROUP_DIM)
    u = rms_norm(u, norm_g.reshape(FN_GROUPS, FN_GROUP_DIM))
    f = jnp.fft.fft2(u.astype(jnp.float32), axes=(1, 3), norm="ortho").real.astype(h.dtype)
    return f.reshape(B, L, FN_WIDTH) @ w_out


def moe_ffn(h, w_router, b_router, w_gu, b_gu, w_down, b_down):
    T, D = h.shape
    logits = (h @ w_router + b_router).astype(jnp.float32)
    top_logit, top_e = lax.top_k(logits, TOP_K)
    gates = jax.nn.softmax(top_logit, axis=-1).astype(h.dtype)
    n_assign = T * TOP_K
    e_flat = top_e.reshape(-1)
    order = jnp.argsort(e_flat)
    e_sorted = e_flat[order]
    tok_sorted = (order // TOP_K).astype(jnp.int32)
    gate_sorted = gates.reshape(-1)[order]
    counts = jnp.bincount(e_flat, length=N_EXPERTS)
    padded = (counts + MOE_BLOCK - 1) // MOE_BLOCK * MOE_BLOCK
    start = jnp.cumsum(counts) - counts
    pend = jnp.cumsum(padded)
    pstart = pend - padded
    dest = pstart[e_sorted] + jnp.arange(n_assign) - start[e_sorted]
    n_blocks = -(-n_assign // MOE_BLOCK) + N_EXPERTS
    cap = n_blocks * MOE_BLOCK
    tok_buf = jnp.zeros((cap,), jnp.int32).at[dest].set(tok_sorted)
    gate_buf = jnp.zeros((cap,), h.dtype).at[dest].set(gate_sorted)
    blk_e = jnp.minimum(jnp.searchsorted(pend, jnp.arange(n_blocks) * MOE_BLOCK, side="right"),
                        N_EXPERTS - 1)

    def expert_block(args):
        ids, e = args
        xb = h[ids]
        gu = xb @ w_gu[e] + b_gu[e]
        gate, up = jnp.split(gu, 2, axis=-1)
        gate = jnp.minimum(gate, SWIGLU_LIMIT)
        up = jnp.clip(up, -SWIGLU_LIMIT, SWIGLU_LIMIT)
        act = gate * jax.nn.sigmoid(gate * SWIGLU_ALPHA) * (up + 1.0)
        return act @ w_down[e] + b_down[e]

    y = lax.map(expert_block, (tok_buf.reshape(n_blocks, MOE_BLOCK), blk_e))
    y = y.reshape(cap, D) * gate_buf[:, None]
    return jax.ops.segment_sum(y, tok_buf, num_segments=T)


def setup_inputs(seed: int = 0) -> dict:
    key = jax.random.key(seed)
    ks = jax.random.split(key, 22)
    nrm = jax.random.normal
    D = D_MODEL
    F = D_EXPERT
    return {
        "x": nrm(ks[0], (BATCH, SEQ, D), jnp.float32),
        "c": nrm(ks[1], (BATCH, D), jnp.float32),
        "ctx": nrm(ks[2], (BATCH, CTX_LEN, D), jnp.float32),
        "c_ctx": nrm(ks[3], (D,), jnp.float32),
        "w_mod": nrm(ks[4], (DEPTH, D, 6 * D), jnp.float32) * (0.5 * D ** -0.5),
        "b_mod": 0.02 * nrm(ks[5], (DEPTH, 6 * D), jnp.float32),
        "ln_g": 1.0 + 0.02 * nrm(ks[6], (DEPTH, 2, D), jnp.float32),
        "ln_b": 0.02 * nrm(ks[7], (DEPTH, 2, D), jnp.float32),
        "attn_w_qkv": nrm(ks[8], (N_ATTN_LAYERS, D, 2 * DA_QK_WIDTH + DA_V_WIDTH), jnp.float32) * D ** -0.5,
        "attn_lambda": 0.1 * nrm(ks[9], (N_ATTN_LAYERS, 4, DA_HEAD_DIM), jnp.float32),
        "attn_subln_g": 1.0 + 0.02 * nrm(ks[10], (N_ATTN_LAYERS, DA_V_DIM), jnp.float32),
        "attn_w_o": nrm(ks[11], (N_ATTN_LAYERS, DA_V_WIDTH, D), jnp.float32) * (DA_V_WIDTH ** -0.5 * DEEPNORM_BETA),
        "fn_w_in": nrm(ks[12], (N_FOURIER_LAYERS, D, FN_WIDTH), jnp.float32) * D ** -0.5,
        "fn_norm_g": 1.0 + 0.02 * nrm(ks[13], (N_FOURIER_LAYERS, FN_WIDTH), jnp.float32),
        "fn_w_out": nrm(ks[14], (N_FOURIER_LAYERS, FN_WIDTH, D), jnp.float32) * (FN_WIDTH ** -0.5 * DEEPNORM_BETA),
        "moe_w_router": nrm(ks[15], (DEPTH, D, N_EXPERTS), jnp.float32) * D ** -0.5,
        "moe_b_router": 0.01 * nrm(ks[16], (DEPTH, N_EXPERTS), jnp.float32),
        "moe_w_gu": nrm(ks[17], (DEPTH, N_EXPERTS, D, 2 * F), jnp.float32) * D ** -0.5,
        "moe_b_gu": 0.02 * nrm(ks[18], (DEPTH, N_EXPERTS, 2 * F), jnp.float32),
        "moe_w_down": nrm(ks[19], (DEPTH, N_EXPERTS, F, D), jnp.float32) * (F ** -0.5 * DEEPNORM_BETA),
        "moe_b_down": 0.02 * nrm(ks[20], (DEPTH, N_EXPERTS, D), jnp.float32),
    }


def reference(x, c, ctx, c_ctx, w_mod, b_mod, ln_g, ln_b, attn_w_qkv, attn_lambda, attn_subln_g,
              attn_w_o, fn_w_in, fn_norm_g, fn_w_out, moe_w_router, moe_b_router, moe_w_gu,
              moe_b_gu, moe_w_down, moe_b_down):
    B, N, D = x.shape
    cos, sin = axial_rope_tables(N, x.dtype)
    last_ctx_layer = max(i for i in range(DEPTH) if i % N_MIXERS == 0)
    for i in range(DEPTH):
        j = i // N_MIXERS
        ctx_full = i < last_ctx_layer
        mod_x = (jax.nn.silu(c) @ w_mod[i] + b_mod[i])[:, None, :]
        sh1, sc1, g1, sh2, sc2, g2 = jnp.split(mod_x, 6, axis=-1)
        if i <= last_ctx_layer:
            mod_c = jax.nn.silu(c_ctx) @ w_mod[i] + b_mod[i]
            csh1, csc1, cg1, csh2, csc2, cg2 = jnp.split(mod_c, 6)

        hx = x * (1.0 + sc1) + sh1
        if i % N_MIXERS == 0:
            hc = ctx * (1.0 + csc1) + csh1
            lambda_init = 0.8 - 0.6 * float(np.exp(-0.3 * i))
            y, y_c = diff_attention(hx, hc, attn_w_qkv[j], attn_lambda[j], attn_subln_g[j],
                                    attn_w_o[j], cos, sin, lambda_init, ctx_full)
        else:
            y = fourier_mix(hx, fn_w_in[j], fn_norm_g[j], fn_w_out[j])
            if ctx_full:
                hc = ctx * (1.0 + csc1) + csh1
                y_c = fourier_mix(hc, fn_w_in[j], fn_norm_g[j], fn_w_out[j])
        x = layer_norm(DEEPNORM_ALPHA * x + g1 * y, ln_g[i, 0], ln_b[i, 0])
        if ctx_full:
            ctx = layer_norm(DEEPNORM_ALPHA * ctx + cg1 * y_c, ln_g[i, 0], ln_b[i, 0])

        hx = x * (1.0 + sc2) + sh2
        if ctx_full:
            hc = ctx * (1.0 + csc2) + csh2
            tokens = jnp.concatenate([hx.reshape(-1, D), hc.reshape(-1, D)], axis=0)
            out = moe_ffn(tokens, moe_w_router[i], moe_b_router[i], moe_w_gu[i], moe_b_gu[i],
                          moe_w_down[i], moe_b_down[i])
            y_x = out[:B * N].reshape(B, N, D)
            y_c = out[B * N:].reshape(B, -1, D)
            ctx = layer_norm(DEEPNORM_ALPHA * ctx + cg2 * y_c, ln_g[i, 1], ln_b[i, 1])
        else:
            y_x = moe_ffn(hx.reshape(-1, D), moe_w_router[i], moe_b_router[i], moe_w_gu[i],
                          moe_b_gu[i], moe_w_down[i], moe_b_down[i]).reshape(B, N, D)
        x = layer_norm(DEEPNORM_ALPHA * x + g2 * y_x, ln_g[i, 1], ln_b[i, 1])
    return x
```

```python
import functools

import jax
import jax.numpy as jnp
from jax import lax
import numpy as np
from jax.experimental import pallas as pl
from jax.experimental.pallas import tpu as pltpu

F32 = jnp.float32
BF16 = jnp.bfloat16
HIGHEST = lax.Precision.HIGHEST

D = 1024
GRID_W = 64
DEPTH = 4
N_MIXERS = 2
HEAD_DIM = 64
HEADS = 8
V_DIM = 128
ROPE_THETA = 10000.0
ROPE_FREQS = HEAD_DIM // 4
FN_GROUPS = 8
FN_GROUP_DIM = D // FN_GROUPS
N_EXPERTS = 32
TOP_K = 4
SWIGLU_LIMIT = 7.0
SWIGLU_ALPHA = 1.702
LN_EPS = 1e-5
RMS_EPS = 1e-5
DEEPNORM_ALPHA = (2 * DEPTH) ** 0.25

LANES = 128
ROW_TILE = 256
ATTN_Q_TILE = 512
ATTN_K_TILE = 512
EXPERT_BLOCK = 512
DFT_TILE = 1024
VMEM_LIMIT = 56 * 1024 * 1024


def _cparams(*sem):
    return pltpu.CompilerParams(dimension_semantics=sem, vmem_limit_bytes=VMEM_LIMIT)


def _layer_norm_rows(z, g, b):
    mu = jnp.mean(z, axis=-1, keepdims=True)
    zc = z - mu
    var = jnp.mean(zc * zc, axis=-1, keepdims=True)
    return zc * lax.rsqrt(var + LN_EPS) * g + b


def _mod_kernel(c_ref, w_ref, b_ref, o_ref):
    c = c_ref[...]
    s = c / (1.0 + jnp.exp(-c))
    o_ref[...] = jnp.dot(s, w_ref[...], precision=HIGHEST, preferred_element_type=F32) + b_ref[...]


def _mod_call(cc, w_mod, b_mod):
    depth, _, width = w_mod.shape
    tn = 1536
    return pl.pallas_call(
        _mod_kernel,
        out_shape=jax.ShapeDtypeStruct((depth, 8, width), F32),
        grid=(depth, width // tn),
        in_specs=[
            pl.BlockSpec((8, D), lambda l, j: (0, 0)),
            pl.BlockSpec((None, D, tn), lambda l, j: (l, 0, j)),
            pl.BlockSpec((None, 1, tn), lambda l, j: (l, 0, j)),
        ],
        out_specs=pl.BlockSpec((None, 8, tn), lambda l, j: (l, 0, j)),
        compiler_params=_cparams("arbitrary", "arbitrary"),
        name="mod",
    )(cc, w_mod, b_mod.reshape(depth, 1, width))


def _qkv_kernel(x_ref, sc_ref, sh_ref, w_ref, cos_ref, sin_ref, q_ref, k_ref, v_ref):
    h = x_ref[...] * (1.0 + sc_ref[...]) + sh_ref[...]
    y = jnp.dot(h.astype(BF16), w_ref[...], preferred_element_type=F32)
    cos = cos_ref[...]
    sin = sin_ref[...]
    lane = lax.broadcasted_iota(jnp.int32, cos.shape, 1)
    first_half = (lane % (2 * ROPE_FREQS)) < ROPE_FREQS

    def rope(t):
        partner = jnp.where(first_half,
                            pltpu.roll(t, LANES - ROPE_FREQS, axis=1),
                            pltpu.roll(t, ROPE_FREQS, axis=1))
        return t * cos + partner * sin

    for j in range(D // LANES):
        cols = slice(j * LANES, (j + 1) * LANES)
        q_ref[:, cols] = (rope(y[:, cols]) * (HEAD_DIM ** -0.5)).astype(BF16)
        k_ref[:, cols] = rope(y[:, D + j * LANES:D + (j + 1) * LANES]).astype(BF16)
    v_ref[...] = y[:, 2 * D:].astype(BF16)


def _qkv_call(xs, sc, sh, w, cos, sin, n_lat, n_ctx, batch):
    t = xs.shape[0]
    tm = ROW_TILE
    lat_tiles = n_lat // tm
    all_lat = batch * lat_tiles

    def mod_idx(i):
        return (jnp.minimum(i // lat_tiles, batch), 0, 0)

    def rope_idx(i):
        return (jnp.where(i < all_lat, i % lat_tiles, lat_tiles), 0)

    def kv_idx(i):
        return (jnp.where(i < all_lat, i // lat_tiles, i - all_lat),
                jnp.where(i < all_lat, i % lat_tiles, lat_tiles), 0)

    kv_shape = jax.ShapeDtypeStruct((batch, n_lat + n_ctx, D), BF16)
    return pl.pallas_call(
        _qkv_kernel,
        out_shape=(jax.ShapeDtypeStruct((t, D), BF16), kv_shape, kv_shape),
        grid=(t // tm,),
        in_specs=[
            pl.BlockSpec((tm, D), lambda i: (i, 0)),
            pl.BlockSpec((None, 1, D), mod_idx),
            pl.BlockSpec((None, 1, D), mod_idx),
            pl.BlockSpec((D, 3 * D), lambda i: (0, 0)),
            pl.BlockSpec((tm, LANES), rope_idx),
            pl.BlockSpec((tm, LANES), rope_idx),
        ],
        out_specs=(
            pl.BlockSpec((tm, D), lambda i: (i, 0)),
            pl.BlockSpec((None, tm, D), kv_idx),
            pl.BlockSpec((None, tm, D), kv_idx),
        ),
        compiler_params=_cparams("arbitrary"),
        name="qkv_rope",
    )(xs, sc, sh, w, cos, sin)


def _rope_tables(n_lat, pad_rows):
    rows = n_lat // GRID_W
    row_ids = jnp.repeat(jnp.arange(rows), GRID_W).astype(F32)
    col_ids = jnp.tile(jnp.arange(GRID_W), rows).astype(F32)
    inv_freq = ROPE_THETA ** (-jnp.arange(ROPE_FREQS, dtype=F32) / ROPE_FREQS)
    ar = row_ids[:, None] * inv_freq
    ac = col_ids[:, None] * inv_freq
    cos = jnp.concatenate([jnp.cos(ar), jnp.cos(ar), jnp.cos(ac), jnp.cos(ac)], axis=1)
    sin = jnp.concatenate([-jnp.sin(ar), jnp.sin(ar), -jnp.sin(ac), jnp.sin(ac)], axis=1)
    reps = LANES // HEAD_DIM
    cos = jnp.tile(cos, (1, reps))
    sin = jnp.tile(sin, (1, reps))
    cos = jnp.concatenate([cos, jnp.ones((pad_rows, LANES), F32)], axis=0)
    sin = jnp.concatenate([sin, jnp.zeros((pad_rows, LANES), F32)], axis=0)
    return cos, sin


def _attn_kernel(lam_ref, g_ref, q_ref, k_ref, v_ref, o_ref, m_ref, l_ref, acc_ref, *,
                 n_keys, k_tile, lambda_init):
    tq = q_ref.shape[0]
    q = q_ref[...]
    lane = lax.broadcasted_iota(jnp.int32, q.shape, 1)
    zero = jnp.zeros_like(q)
    qq = jnp.concatenate([jnp.where(lane < HEAD_DIM, q, zero),
                          jnp.where(lane >= HEAD_DIM, q, zero)], axis=0)

    m_ref[...] = jnp.full(m_ref.shape, -jnp.inf, F32)
    l_ref[...] = jnp.zeros(l_ref.shape, F32)
    acc_ref[...] = jnp.zeros(acc_ref.shape, F32)

    def step(start, size):
        kj = k_ref[pl.ds(start, size), :]
        vj = v_ref[pl.ds(start, size), :]
        s = lax.dot_general(qq, kj, (((1,), (1,)), ((), ())), preferred_element_type=F32)
        m_old = m_ref[...]
        m_new = jnp.maximum(m_old, jnp.max(s, axis=1, keepdims=True))
        alpha = jnp.exp(m_old - m_new)
        p = jnp.exp(s - m_new)
        l_ref[...] = alpha * l_ref[...] + jnp.sum(p, axis=1, keepdims=True)
        acc_ref[...] = alpha * acc_ref[...] + jnp.dot(p.astype(BF16), vj, preferred_element_type=F32)
        m_ref[...] = m_new

    n_full = n_keys // k_tile
    if n_full > 0:
        def body(j, carry):
            step(pl.multiple_of(j * k_tile, k_tile), k_tile)
            return carry
        lax.fori_loop(0, n_full, body, 0)
    if n_keys % k_tile:
        step(n_full * k_tile, n_keys % k_tile)

    lam = lam_ref[...]
    lam_val = (jnp.exp(jnp.sum(lam[0:1] * lam[1:2], axis=1, keepdims=True))
               - jnp.exp(jnp.sum(lam[2:3] * lam[3:4], axis=1, keepdims=True)) + lambda_init)
    o_all = acc_ref[...] / l_ref[...]
    o = o_all[:tq] - lam_val * o_all[tq:]
    o = o * lax.rsqrt(jnp.mean(o * o, axis=-1, keepdims=True) + RMS_EPS)
    o_ref[...] = (o * g_ref[...] * (1.0 - lambda_init)).astype(o_ref.dtype)


def _attn_call(q, k_all, v_all, lam, subln_g, lambda_init, *, batch, q_rows, q_row0, tq,
               key_row0, n_keys):
    nq = q_rows // tq
    q_blk0 = q_row0 // tq
    key_blk = key_row0 // n_keys
    kern = functools.partial(_attn_kernel, n_keys=n_keys, k_tile=ATTN_K_TILE,
                             lambda_init=lambda_init)
    return pl.pallas_call(
        kern,
        out_shape=jax.ShapeDtypeStruct((batch * q_rows, D), BF16),
        grid=(batch, HEADS, nq),
        in_specs=[
            pl.BlockSpec((4, HEAD_DIM), lambda b, h, i: (0, 0)),
            pl.BlockSpec((1, V_DIM), lambda b, h, i: (0, 0)),
            pl.BlockSpec((tq, V_DIM), lambda b, h, i: (q_blk0 + b * nq + i, h)),
            pl.BlockSpec((None, n_keys, V_DIM), lambda b, h, i: (b, key_blk, h)),
            pl.BlockSpec((None, n_keys, V_DIM), lambda b, h, i: (b, key_blk, h)),
        ],
        out_specs=pl.BlockSpec((tq, V_DIM), lambda b, h, i: (b * nq + i, h)),
        scratch_shapes=[
            pltpu.VMEM((2 * tq, 1), F32),
            pltpu.VMEM((2 * tq, 1), F32),
            pltpu.VMEM((2 * tq, V_DIM), F32),
        ],
        compiler_params=_cparams("arbitrary", "arbitrary", "arbitrary"),
        name="diff_attn",
    )(lam, subln_g.reshape(1, V_DIM), q, k_all, v_all)


def _proj_ln_kernel(a_ref, w_ref, x_ref, gate_ref, lng_ref, lnb_ref, o_ref):
    y = jnp.dot(a_ref[...], w_ref[...], preferred_element_type=F32)
    z = DEEPNORM_ALPHA * x_ref[...] + gate_ref[...] * y
    o_ref[...] = _layer_norm_rows(z, lng_ref[...], lnb_ref[...])


def _proj_ln_call(a, w, xs, gate, ln_g, ln_b, mod_idx):
    t = xs.shape[0]
    tm = ROW_TILE
    return pl.pallas_call(
        _proj_ln_kernel,
        out_shape=jax.ShapeDtypeStruct((t, D), F32),
        grid=(t // tm,),
        in_specs=[
            pl.BlockSpec((tm, D), lambda i: (i, 0)),
            pl.BlockSpec((D, D), lambda i: (0, 0)),
            pl.BlockSpec((tm, D), lambda i: (i, 0)),
            pl.BlockSpec((None, 1, D), mod_idx),
            pl.BlockSpec((1, D), lambda i: (0, 0)),
            pl.BlockSpec((1, D), lambda i: (0, 0)),
        ],
        out_specs=pl.BlockSpec((tm, D), lambda i: (i, 0)),
        compiler_params=_cparams("arbitrary"),
        name="proj_ln",
    )(a, w, xs, gate, ln_g.reshape(1, D), ln_b.reshape(1, D))


def _fn_in_kernel(x_ref, sc_ref, sh_ref, w_ref, g_ref, cs_ref, a_ref, b_ref):
    h = x_ref[...] * (1.0 + sc_ref[...]) + sh_ref[...]
    u = jnp.dot(h.astype(BF16), w_ref[...], preferred_element_type=F32)
    gain = g_ref[...]
    cs = cs_ref[...]
    for g in range(FN_GROUPS):
        cols = slice(g * FN_GROUP_DIM, (g + 1) * FN_GROUP_DIM)
        ug = u[:, cols]
        r = ug * lax.rsqrt(jnp.mean(ug * ug, axis=-1, keepdims=True) + RMS_EPS) * gain[:, cols]
        ab = jnp.dot(r.astype(BF16), cs, preferred_element_type=F32)
        a_ref[:, cols] = ab[:, :FN_GROUP_DIM].astype(BF16)
        b_ref[:, cols] = ab[:, FN_GROUP_DIM:].astype(BF16)


def _fn_in_call(xs, sc, sh, w, norm_g, cs, mod_idx):
    t = xs.shape[0]
    tm = ROW_TILE
    out = jax.ShapeDtypeStruct((t, D), BF16)
    return pl.pallas_call(
        _fn_in_kernel,
        out_shape=(out, out),
        grid=(t // tm,),
        in_specs=[
            pl.BlockSpec((tm, D), lambda i: (i, 0)),
            pl.BlockSpec((None, 1, D), mod_idx),
            pl.BlockSpec((None, 1, D), mod_idx),
            pl.BlockSpec((D, D), lambda i: (0, 0)),
            pl.BlockSpec((1, D), lambda i: (0, 0)),
            pl.BlockSpec((FN_GROUP_DIM, 2 * FN_GROUP_DIM), lambda i: (0, 0)),
        ],
        out_specs=(pl.BlockSpec((tm, D), lambda i: (i, 0)),
                   pl.BlockSpec((tm, D), lambda i: (i, 0))),
        compiler_params=_cparams("arbitrary"),
        name="fn_in",
    )(xs, sc, sh, w, norm_g.reshape(1, D), cs)


def _dft_tables(n):
    idx = jnp.arange(n, dtype=jnp.int32)
    prod = (idx[:, None] * idx[None, :]) % n
    ang = prod.astype(F32) * (2.0 * np.pi / n)
    return jnp.cos(ang), jnp.sin(ang)


def _dft_kernel(c_ref, s_ref, a0_ref, a1_ref, b0_ref, b1_ref, o_ref, acc_ref, *, scale):
    kj = pl.program_id(1)

    @pl.when(kj == 0)
    def _():
        acc_ref[...] = jnp.zeros(acc_ref.shape, F32)

    c = c_ref[...]
    s = s_ref[...]
    acc_ref[0] += (jnp.dot(c, a0_ref[...], preferred_element_type=F32)
                   - jnp.dot(s, b0_ref[...], preferred_element_type=F32))
    acc_ref[1] += (jnp.dot(c, a1_ref[...], preferred_element_type=F32)
                   - jnp.dot(s, b1_ref[...], preferred_element_type=F32))

    @pl.when(kj == pl.num_programs(1) - 1)
    def _():
        o_ref[...] = (acc_ref[...] * scale).astype(o_ref.dtype)


def _dft_call(ct, st, a, bp, *, seq, row0, tile):
    nt = seq // tile
    blk0 = row0 // tile
    kern = functools.partial(_dft_kernel, scale=float((seq * FN_GROUP_DIM) ** -0.5))
    seq_spec0 = pl.BlockSpec((tile, D), lambda i, j: (blk0 + j, 0))
    seq_spec1 = pl.BlockSpec((tile, D), lambda i, j: (blk0 + nt + j, 0))
    return pl.pallas_call(
        kern,
        out_shape=jax.ShapeDtypeStruct((2, seq, D), BF16),
        grid=(nt, nt),
        in_specs=[
            pl.BlockSpec((tile, tile), lambda i, j: (i, j)),
            pl.BlockSpec((tile, tile), lambda i, j: (i, j)),
            seq_spec0, seq_spec1, seq_spec0, seq_spec1,
        ],
        out_specs=pl.BlockSpec((2, tile, D), lambda i, j: (0, i, 0)),
        scratch_shapes=[pltpu.VMEM((2, tile, D), F32)],
        compiler_params=_cparams("arbitrary", "arbitrary"),
        name="seq_dft",
    )(ct, st, a, a, bp, bp)


def _router_kernel(x_ref, sc_ref, sh_ref, w_ref, b_ref, h_ref, meta_ref, gate_ref, cnt_ref,
                   carry_ref):
    i = pl.program_id(0)

    @pl.when(i == 0)
    def _():
        carry_ref[...] = jnp.zeros(carry_ref.shape, F32)

    h = x_ref[...] * (1.0 + sc_ref[...]) + sh_ref[...]
    h_ref[...] = h
    logits = jnp.dot(h, w_ref[...], precision=HIGHEST, preferred_element_type=F32) + b_ref[...]
    tm = logits.shape[0]
    lane = lax.broadcasted_iota(jnp.int32, logits.shape, 1).astype(F32)
    wide = lax.broadcasted_iota(jnp.int32, (tm, LANES), 1)

    work = logits
    onehot = jnp.zeros(logits.shape, F32)
    picks, vals = [], []
    for _ in range(TOP_K):
        mx = jnp.max(work, axis=1, keepdims=True)
        idx = jnp.min(jnp.where(work == mx, lane, float(N_EXPERTS)), axis=1, keepdims=True)
        sel = lane == idx
        onehot = onehot + sel.astype(F32)
        work = jnp.where(sel, -jnp.inf, work)
        picks.append(idx)
        vals.append(mx)

    exps = [jnp.exp(v - vals[0]) for v in vals]
    denom = exps[0] + exps[1] + exps[2] + exps[3]

    r_i = lax.broadcasted_iota(jnp.int32, (tm, tm), 0)
    c_i = lax.broadcasted_iota(jnp.int32, (tm, tm), 1)
    tri = (c_i < r_i).astype(BF16)
    prefix = jnp.dot(tri, onehot.astype(BF16), preferred_element_type=F32) + carry_ref[0:1, 0:N_EXPERTS]

    meta = jnp.zeros((tm, LANES), jnp.int32)
    gates = jnp.zeros((tm, LANES), F32)
    for k in range(TOP_K):
        rank = jnp.sum(jnp.where(lane == picks[k], prefix, 0.0), axis=1, keepdims=True)
        meta = jnp.where(wide == k, picks[k].astype(jnp.int32), meta)
        meta = jnp.where(wide == TOP_K + k, rank.astype(jnp.int32), meta)
        gates = jnp.where(wide == k, exps[k] / denom, gates)
    meta_ref[...] = meta
    gate_ref[...] = gates

    total = carry_ref[0:1, 0:N_EXPERTS] + jnp.sum(onehot, axis=0, keepdims=True)
    carry_ref[0:1, 0:N_EXPERTS] = total
    cnt_ref[...] = jnp.zeros(cnt_ref.shape, jnp.int32)
    cnt_ref[0:1, 0:N_EXPERTS] = total.astype(jnp.int32)


def _router_call(xs, sc, sh, w, b, mod_idx):
    t = xs.shape[0]
    tm = ROW_TILE
    return pl.pallas_call(
        _router_kernel,
        out_shape=(jax.ShapeDtypeStruct((t, D), F32),
                   jax.ShapeDtypeStruct((t, LANES), jnp.int32),
                   jax.ShapeDtypeStruct((t, LANES), F32),
                   jax.ShapeDtypeStruct((8, LANES), jnp.int32)),
        grid=(t // tm,),
        in_specs=[
            pl.BlockSpec((tm, D), lambda i: (i, 0)),
            pl.BlockSpec((None, 1, D), mod_idx),
            pl.BlockSpec((None, 1, D), mod_idx),
            pl.BlockSpec((D, N_EXPERTS), lambda i: (0, 0)),
            pl.BlockSpec((1, N_EXPERTS), lambda i: (0, 0)),
        ],
        out_specs=(pl.BlockSpec((tm, D), lambda i: (i, 0)),
                   pl.BlockSpec((tm, LANES), lambda i: (i, 0)),
                   pl.BlockSpec((tm, LANES), lambda i: (i, 0)),
                   pl.BlockSpec((8, LANES), lambda i: (0, 0))),
        scratch_shapes=[pltpu.VMEM((8, LANES), F32)],
        compiler_params=_cparams("arbitrary"),
        name="router",
    )(xs, sc, sh, w, b.reshape(1, N_EXPERTS))


def _dispatch_kernel(dest_ref, h_ref, init_ref, xs_ref, sem, *, chunk):
    del init_ref
    t0 = pl.program_id(0) * chunk

    def row_copy(t, d):
        return pltpu.make_async_copy(h_ref.at[pl.ds(t, 1)], xs_ref.at[pl.ds(d, 1)], sem)

    def issue(r, carry):
        t = t0 + r
        for k in range(TOP_K):
            row_copy(t, dest_ref[t * TOP_K + k]).start()
        return carry
    lax.fori_loop(0, chunk, issue, 0)

    def drain(r, carry):
        row_copy(0, 0).wait()
        return carry
    lax.fori_loop(0, chunk * TOP_K, drain, 0)


def _dispatch_call(dest, h, cap):
    t = h.shape[0]
    kern = functools.partial(_dispatch_kernel, chunk=ROW_TILE)
    return pl.pallas_call(
        kern,
        out_shape=jax.ShapeDtypeStruct((cap, D), F32),
        grid_spec=pltpu.PrefetchScalarGridSpec(
            num_scalar_prefetch=1,
            grid=(t // ROW_TILE,),
            in_specs=[pl.BlockSpec(memory_space=pl.ANY), pl.BlockSpec(memory_space=pl.ANY)],
            out_specs=pl.BlockSpec(memory_space=pl.ANY),
            scratch_shapes=[pltpu.SemaphoreType.DMA],
        ),
        input_output_aliases={2: 0},
        compiler_params=_cparams("arbitrary"),
        name="moe_dispatch",
    )(dest, h, jnp.zeros((cap, D), F32))


def _expert_kernel(blk_e_ref, n_used_ref, x_ref, wgu_ref, bgu_ref, wd_ref, bd_ref, y_ref):
    del blk_e_ref
    i = pl.program_id(0)
    f = wd_ref.shape[0]

    @pl.when(i < n_used_ref[0])
    def _():
        xb = x_ref[...].astype(BF16)
        gu = jnp.dot(xb, wgu_ref[...], preferred_element_type=F32) + bgu_ref[...]
        gate = jnp.minimum(gu[:, :f], SWIGLU_LIMIT)
        up = jnp.clip(gu[:, f:], -SWIGLU_LIMIT, SWIGLU_LIMIT)
        act = gate / (1.0 + jnp.exp(-SWIGLU_ALPHA * gate)) * (up + 1.0)
        y_ref[...] = jnp.dot(act.astype(BF16), wd_ref[...], preferred_element_type=F32) + bd_ref[...]

    @pl.when(i >= n_used_ref[0])
    def _():
        y_ref[...] = jnp.zeros(y_ref.shape, F32)


def _expert_call(blk_e, n_used, xs, w_gu, b_gu, w_down, b_down):
    cap = xs.shape[0]
    blk = EXPERT_BLOCK
    e, _, f2 = w_gu.shape
    f = w_down.shape[1]
    return pl.pallas_call(
        _expert_kernel,
        out_shape=jax.ShapeDtypeStruct((cap, D), F32),
        grid_spec=pltpu.PrefetchScalarGridSpec(
            num_scalar_prefetch=2,
            grid=(cap // blk,),
            in_specs=[
                pl.BlockSpec((blk, D), lambda i, be, nu: (i, 0)),
                pl.BlockSpec((None, D, f2), lambda i, be, nu: (be[i], 0, 0)),
                pl.BlockSpec((None, 1, f2), lambda i, be, nu: (be[i], 0, 0)),
                pl.BlockSpec((None, f, D), lambda i, be, nu: (be[i], 0, 0)),
                pl.BlockSpec((None, 1, D), lambda i, be, nu: (be[i], 0, 0)),
            ],
            out_specs=pl.BlockSpec((blk, D), lambda i, be, nu: (i, 0)),
        ),
        compiler_params=_cparams("arbitrary"),
        name="moe_experts",
    )(blk_e, n_used, xs, w_gu, b_gu.reshape(e, 1, f2), w_down, b_down.reshape(e, 1, D))


def _combine_kernel(dest_ref, y_ref, gates_ref, x_ref, g2_ref, lng_ref, lnb_ref, o_ref, buf, sem):
    i = pl.program_id(0)
    tm = x_ref.shape[0]

    def row_copy(r, k, d):
        return pltpu.make_async_copy(y_ref.at[pl.ds(d, 1)], buf.at[k, pl.ds(r, 1)], sem)

    def issue(r, carry):
        base = (i * tm + r) * TOP_K
        for k in range(TOP_K):
            row_copy(r, k, dest_ref[base + k]).start()
        return carry
    lax.fori_loop(0, tm, issue, 0)

    def drain(r, carry):
        row_copy(0, 0, 0).wait()
        return carry
    lax.fori_loop(0, tm * TOP_K, drain, 0)

    gates = gates_ref[...]
    y = gates[:, 0:1] * buf[0]
    for k in range(1, TOP_K):
        y = y + gates[:, k:k + 1] * buf[k]
    z = DEEPNORM_ALPHA * x_ref[...] + g2_ref[...] * y
    o_ref[...] = _layer_norm_rows(z, lng_ref[...], lnb_ref[...])


def _combine_call(dest, y, gates, xs, g2, ln_g, ln_b, lat_tiles, batch):
    t = xs.shape[0]
    tm = ROW_TILE

    def mod_idx(i, d):
        return (jnp.minimum(i // lat_tiles, batch), 0, 0)

    return pl.pallas_call(
        _combine_kernel,
        out_shape=jax.ShapeDtypeStruct((t, D), F32),
        grid_spec=pltpu.PrefetchScalarGridSpec(
            num_scalar_prefetch=1,
            grid=(t // tm,),
            in_specs=[
                pl.BlockSpec(memory_space=pl.ANY),
                pl.BlockSpec((tm, LANES), lambda i, d: (i, 0)),
                pl.BlockSpec((tm, D), lambda i, d: (i, 0)),
                pl.BlockSpec((None, 1, D), mod_idx),
                pl.BlockSpec((1, D), lambda i, d: (0, 0)),
                pl.BlockSpec((1, D), lambda i, d: (0, 0)),
            ],
            out_specs=pl.BlockSpec((tm, D), lambda i, d: (i, 0)),
            scratch_shapes=[pltpu.VMEM((TOP_K, tm, D), F32), pltpu.SemaphoreType.DMA],
        ),
        compiler_params=_cparams("arbitrary"),
        name="moe_combine_ln",
    )(dest, y, gates, xs, g2, ln_g.reshape(1, D), ln_b.reshape(1, D))


def _moe_plan(meta, counts, cap):
    blk = EXPERT_BLOCK
    ids = meta[:, :TOP_K]
    rank = meta[:, TOP_K:2 * TOP_K]
    cnt = counts[0, :N_EXPERTS]
    padded = (cnt + blk - 1) // blk * blk
    pend = jnp.cumsum(padded)
    pstart = pend - padded
    dest = (pstart[ids] + rank).reshape(-1).astype(jnp.int32)
    n_blocks = cap // blk
    blk_e = jnp.minimum(jnp.searchsorted(pend, jnp.arange(n_blocks) * blk, side="right"),
                        N_EXPERTS - 1).astype(jnp.int32)
    n_used = (pend[-1:] // blk).astype(jnp.int32)
    return dest, blk_e, n_used


def kernel(x, c, ctx, c_ctx, w_mod, b_mod, ln_g, ln_b, attn_w_qkv, attn_lambda, attn_subln_g,
           attn_w_o, fn_w_in, fn_norm_g, fn_w_out, moe_w_router, moe_b_router, moe_w_gu,
           moe_b_gu, moe_w_down, moe_b_down):
    batch, n_lat, d = x.shape
    n_ctx = ctx.shape[1]
    assert d == D and batch == 2 and n_lat % DFT_TILE == 0 and n_ctx == ROW_TILE
    t_lat = batch * n_lat
    t_all = t_lat + batch * n_ctx
    lat_tiles = n_lat // ROW_TILE
    last_ctx_layer = max(i for i in range(DEPTH) if i % N_MIXERS == 0)

    def mod_idx(i):
        return (jnp.minimum(i // lat_tiles, batch), 0, 0)

    xs = jnp.concatenate([x.reshape(t_lat, D), ctx.reshape(batch * n_ctx, D)], axis=0)
    cc = jnp.concatenate([c, c_ctx[None, :], jnp.zeros((8 - batch - 1, D), F32)], axis=0)
    mods = _mod_call(cc, w_mod, b_mod)

    cos, sin = _rope_tables(n_lat, ROW_TILE)
    cs = jnp.concatenate(_dft_tables(FN_GROUP_DIM), axis=1).astype(BF16)
    ct_lat, st_lat = [m.astype(BF16) for m in _dft_tables(n_lat)]
    ct_ctx, st_ctx = [m.astype(BF16) for m in _dft_tables(n_ctx)]

    n_assign = t_all * TOP_K
    cap = -(-n_assign // EXPERT_BLOCK) * EXPERT_BLOCK + N_EXPERTS * EXPERT_BLOCK

    for i in range(DEPTH):
        j = i // N_MIXERS
        ctx_full = i < last_ctx_layer

        def mvec(k, i=i):
            return mods[i, :batch + 1, k * D:(k + 1) * D].reshape(batch + 1, 1, D)

        sh1, sc1, g1, sh2, sc2, g2 = [mvec(k) for k in range(6)]

        if i % N_MIXERS == 0:
            lambda_init = 0.8 - 0.6 * float(np.exp(-0.3 * i))
            q, k_all, v_all = _qkv_call(xs, sc1, sh1, attn_w_qkv[j].astype(BF16), cos, sin,
                                        n_lat, n_ctx, batch)
            o_lat = _attn_call(q, k_all, v_all, attn_lambda[j], attn_subln_g[j], lambda_init,
                               batch=batch, q_rows=n_lat, q_row0=0, tq=ATTN_Q_TILE,
                               key_row0=0, n_keys=n_lat + n_ctx)
            if ctx_full:
                o_ctx = _attn_call(q, k_all, v_all, attn_lambda[j], attn_subln_g[j], lambda_init,
                                   batch=batch, q_rows=n_ctx, q_row0=t_lat, tq=n_ctx,
                                   key_row0=n_lat, n_keys=n_ctx)
            else:
                o_ctx = jnp.zeros((batch * n_ctx, D), BF16)
            branch = jnp.concatenate([o_lat, o_ctx], axis=0)
            w_out = attn_w_o[j]
        else:
            a, bp = _fn_in_call(xs, sc1, sh1, fn_w_in[j].astype(BF16), fn_norm_g[j], cs, mod_idx)
            f_lat = _dft_call(ct_lat, st_lat, a, bp, seq=n_lat, row0=0, tile=DFT_TILE)
            if ctx_full:
                f_ctx = _dft_call(ct_ctx, st_ctx, a, bp, seq=n_ctx, row0=t_lat, tile=n_ctx)
            else:
                f_ctx = jnp.zeros((batch, n_ctx, D), BF16)
            branch = jnp.concatenate([f_lat.reshape(t_lat, D), f_ctx.reshape(batch * n_ctx, D)],
                                     axis=0)
            w_out = fn_w_out[j]
        xs = _proj_ln_call(branch, w_out.astype(BF16), xs, g1, ln_g[i, 0], ln_b[i, 0], mod_idx)

        h2, meta, gates, counts = _router_call(xs, sc2, sh2, moe_w_router[i], moe_b_router[i],
                                               mod_idx)
        dest, blk_e, n_used = _moe_plan(meta, counts, cap)
        xd = _dispatch_call(dest, h2, cap)
        y = _expert_call(blk_e, n_used, xd, moe_w_gu[i].astype(BF16), moe_b_gu[i],
                         moe_w_down[i].astype(BF16), moe_b_down[i])
        xs = _combine_call(dest, y, gates, xs, g2, ln_g[i, 1], ln_b[i, 1], lat_tiles, batch)

    return xs[:t_lat].reshape(batch, n_lat, D)
```

```python
import functools

import jax
import jax.numpy as jnp
from jax import lax
import numpy as np
from jax.experimental import pallas as pl
from jax.experimental.pallas import tpu as pltpu

F32 = jnp.float32
BF16 = jnp.bfloat16
HIGHEST = lax.Precision.HIGHEST

D = 1024
GRID_W = 64
DEPTH = 4
N_MIXERS = 2
HEAD_DIM = 64
HEADS = 8
V_DIM = 128
ROPE_THETA = 10000.0
ROPE_FREQS = HEAD_DIM // 4
FN_GROUPS = 8
FN_GROUP_DIM = D // FN_GROUPS
N_EXPERTS = 32
TOP_K = 4
SWIGLU_LIMIT = 7.0
SWIGLU_ALPHA = 1.702
LN_EPS = 1e-5
RMS_EPS = 1e-5
DEEPNORM_ALPHA = (2 * DEPTH) ** 0.25

LANES = 128
ROW_TILE = 256
ATTN_Q_TILE = 512
ATTN_K_TILE = 512
EXPERT_BLOCK = 512
Q_SCALE = HEAD_DIM ** -0.5 * float(np.log2(np.e))
DFT_TILE = 1024
VMEM_LIMIT = 56 * 1024 * 1024


def _cparams(*sem):
    return pltpu.CompilerParams(dimension_semantics=sem, vmem_limit_bytes=VMEM_LIMIT)


def _layer_norm_rows(z, g, b):
    mu = jnp.mean(z, axis=-1, keepdims=True)
    zc = z - mu
    var = jnp.mean(zc * zc, axis=-1, keepdims=True)
    return zc * lax.rsqrt(var + LN_EPS) * g + b


def _mod_kernel(c_ref, w_ref, b_ref, o_ref):
    c = c_ref[...]
    s = c / (1.0 + jnp.exp(-c))
    o_ref[...] = jnp.dot(s, w_ref[...], precision=HIGHEST, preferred_element_type=F32) + b_ref[...]


def _mod_call(cc, w_mod, b_mod):
    depth, _, width = w_mod.shape
    tn = 1536
    return pl.pallas_call(
        _mod_kernel,
        out_shape=jax.ShapeDtypeStruct((depth, 8, width), F32),
        grid=(depth, width // tn),
        in_specs=[
            pl.BlockSpec((8, D), lambda l, j: (0, 0)),
            pl.BlockSpec((None, D, tn), lambda l, j: (l, 0, j)),
            pl.BlockSpec((None, 1, tn), lambda l, j: (l, 0, j)),
        ],
        out_specs=pl.BlockSpec((None, 8, tn), lambda l, j: (l, 0, j)),
        compiler_params=_cparams("arbitrary", "arbitrary"),
        name="mod",
    )(cc, w_mod, b_mod.reshape(depth, 1, width))


def _qkv_kernel(x_ref, sc_ref, sh_ref, w_ref, cos_ref, sin_ref, q_ref, k_ref, vt_ref):
    h = x_ref[...] * (1.0 + sc_ref[...]) + sh_ref[...]
    y = jnp.dot(h.astype(BF16), w_ref[...], preferred_element_type=F32)
    cos = cos_ref[...]
    sin = sin_ref[...]
    lane = lax.broadcasted_iota(jnp.int32, cos.shape, 1)
    first_half = (lane % (2 * ROPE_FREQS)) < ROPE_FREQS

    def rope(t):
        partner = jnp.where(first_half,
                            pltpu.roll(t, LANES - ROPE_FREQS, axis=1),
                            pltpu.roll(t, ROPE_FREQS, axis=1))
        return t * cos + partner * sin

    for j in range(D // LANES):
        cols = slice(j * LANES, (j + 1) * LANES)
        q_ref[:, cols] = (rope(y[:, cols]) * Q_SCALE).astype(BF16)
        k_ref[:, cols] = rope(y[:, D + j * LANES:D + (j + 1) * LANES]).astype(BF16)
    vt_ref[...] = y[:, 2 * D:].T.astype(BF16)


def _qkv_call(xs, sc, sh, w, cos, sin, n_lat, n_ctx, batch):
    t = xs.shape[0]
    tm = ROW_TILE
    lat_tiles = n_lat // tm
    all_lat = batch * lat_tiles

    def mod_idx(i):
        return (jnp.minimum(i // lat_tiles, batch), 0, 0)

    def rope_idx(i):
        return (jnp.where(i < all_lat, i % lat_tiles, lat_tiles), 0)

    def kv_idx(i):
        return (jnp.where(i < all_lat, i // lat_tiles, i - all_lat),
                jnp.where(i < all_lat, i % lat_tiles, lat_tiles), 0)

    def vt_idx(i):
        return kv_idx(i)[:2] + (0, 0)

    return pl.pallas_call(
        _qkv_kernel,
        out_shape=(jax.ShapeDtypeStruct((t, D), BF16),
                   jax.ShapeDtypeStruct((batch, n_lat + n_ctx, D), BF16),
                   jax.ShapeDtypeStruct((batch, lat_tiles + 1, D, tm), BF16)),
        grid=(t // tm,),
        in_specs=[
            pl.BlockSpec((tm, D), lambda i: (i, 0)),
            pl.BlockSpec((None, 1, D), mod_idx),
            pl.BlockSpec((None, 1, D), mod_idx),
            pl.BlockSpec((D, 3 * D), lambda i: (0, 0)),
            pl.BlockSpec((tm, LANES), rope_idx),
            pl.BlockSpec((tm, LANES), rope_idx),
        ],
        out_specs=(
            pl.BlockSpec((tm, D), lambda i: (i, 0)),
            pl.BlockSpec((None, tm, D), kv_idx),
            pl.BlockSpec((None, None, D, tm), vt_idx),
        ),
        compiler_params=_cparams("arbitrary"),
        name="qkv_rope",
    )(xs, sc, sh, w, cos, sin)


def _rope_tables(n_lat, pad_rows):
    rows = n_lat // GRID_W
    row_ids = jnp.repeat(jnp.arange(rows), GRID_W).astype(F32)
    col_ids = jnp.tile(jnp.arange(GRID_W), rows).astype(F32)
    inv_freq = ROPE_THETA ** (-jnp.arange(ROPE_FREQS, dtype=F32) / ROPE_FREQS)
    ar = row_ids[:, None] * inv_freq
    ac = col_ids[:, None] * inv_freq
    cos = jnp.concatenate([jnp.cos(ar), jnp.cos(ar), jnp.cos(ac), jnp.cos(ac)], axis=1)
    sin = jnp.concatenate([-jnp.sin(ar), jnp.sin(ar), -jnp.sin(ac), jnp.sin(ac)], axis=1)
    reps = LANES // HEAD_DIM
    cos = jnp.tile(cos, (1, reps))
    sin = jnp.tile(sin, (1, reps))
    cos = jnp.concatenate([cos, jnp.ones((pad_rows, LANES), F32)], axis=0)
    sin = jnp.concatenate([sin, jnp.zeros((pad_rows, LANES), F32)], axis=0)
    return cos, sin


def _attn_kernel(lam_ref, g_ref, q_ref, k_ref, vt_ref, o_ref, m_ref, l_ref, acc_ref, *,
                 n_chunks, chunks_per_step, lambda_init):
    tq = q_ref.shape[0]
    kc = vt_ref.shape[2]
    qt = q_ref[...].astype(F32).T
    sub = lax.broadcasted_iota(jnp.int32, qt.shape, 0)
    qq = jnp.concatenate([jnp.where(sub < HEAD_DIM, qt, 0.0),
                          jnp.where(sub >= HEAD_DIM, qt, 0.0)], axis=1).astype(BF16)

    m_ref[...] = jnp.full(m_ref.shape, -jnp.inf, F32)
    l_ref[...] = jnp.zeros(l_ref.shape, F32)
    acc_ref[...] = jnp.zeros(acc_ref.shape, F32)

    def step(c0, nc):
        kj = k_ref[pl.ds(c0 * kc, nc * kc), :]
        s = jnp.dot(kj, qq, preferred_element_type=F32)
        m_old = m_ref[...]
        m_new = jnp.maximum(m_old, jnp.max(s, axis=0, keepdims=True))
        alpha = jnp.exp2(m_old - m_new)
        p = jnp.exp2(s - m_new)
        l_ref[...] = alpha * l_ref[...] + jnp.sum(p, axis=0, keepdims=True)
        pb = p.astype(BF16)
        pv = jnp.dot(vt_ref[c0], pb[0:kc], preferred_element_type=F32)
        for c in range(1, nc):
            pv = pv + jnp.dot(vt_ref[c0 + c], pb[c * kc:(c + 1) * kc],
                              preferred_element_type=F32)
        acc_ref[...] = alpha * acc_ref[...] + pv
        m_ref[...] = m_new

    n_full = n_chunks // chunks_per_step
    if n_full > 0:
        def body(j, carry):
            step(j * chunks_per_step, chunks_per_step)
            return carry
        lax.fori_loop(0, n_full, body, 0)
    if n_chunks % chunks_per_step:
        step(n_full * chunks_per_step, n_chunks % chunks_per_step)

    lam = lam_ref[...]
    lam_val = (jnp.exp(jnp.sum(lam[0:1] * lam[1:2], axis=1, keepdims=True))
               - jnp.exp(jnp.sum(lam[2:3] * lam[3:4], axis=1, keepdims=True)) + lambda_init)
    o_all = acc_ref[...] / l_ref[...]
    o = (o_all[:, :tq] - lam_val * o_all[:, tq:]).T
    o = o * lax.rsqrt(jnp.mean(o * o, axis=-1, keepdims=True) + RMS_EPS)
    o_ref[...] = (o * g_ref[...] * (1.0 - lambda_init)).astype(o_ref.dtype)


def _attn_call(q, k_all, vt_all, lam, subln_g, lambda_init, *, batch, q_rows, q_row0, tq,
               chunk0, n_chunks):
    nq = q_rows // tq
    q_blk0 = q_row0 // tq
    kc = vt_all.shape[3]
    key_blk = chunk0 // n_chunks
    kern = functools.partial(_attn_kernel, n_chunks=n_chunks,
                             chunks_per_step=ATTN_K_TILE // kc, lambda_init=lambda_init)
    return pl.pallas_call(
        kern,
        out_shape=jax.ShapeDtypeStruct((batch * q_rows, D), BF16),
        grid=(batch, HEADS, nq),
        in_specs=[
            pl.BlockSpec((4, HEAD_DIM), lambda b, h, i: (0, 0)),
            pl.BlockSpec((1, V_DIM), lambda b, h, i: (0, 0)),
            pl.BlockSpec((tq, V_DIM), lambda b, h, i: (q_blk0 + b * nq + i, h)),
            pl.BlockSpec((None, n_chunks * kc, V_DIM), lambda b, h, i: (b, key_blk, h)),
            pl.BlockSpec((None, n_chunks, V_DIM, kc), lambda b, h, i: (b, key_blk, h, 0)),
        ],
        out_specs=pl.BlockSpec((tq, V_DIM), lambda b, h, i: (b * nq + i, h)),
        scratch_shapes=[
            pltpu.VMEM((1, 2 * tq), F32),
            pltpu.VMEM((1, 2 * tq), F32),
            pltpu.VMEM((V_DIM, 2 * tq), F32),
        ],
        compiler_params=_cparams("arbitrary", "arbitrary", "arbitrary"),
        name="diff_attn",
    )(lam, subln_g.reshape(1, V_DIM), q, k_all, vt_all)


def _proj_ln_kernel(a_ref, w_ref, x_ref, gate_ref, lng_ref, lnb_ref, o_ref):
    y = jnp.dot(a_ref[...], w_ref[...], preferred_element_type=F32)
    z = DEEPNORM_ALPHA * x_ref[...] + gate_ref[...] * y
    o_ref[...] = _layer_norm_rows(z, lng_ref[...], lnb_ref[...])


def _proj_ln_call(a, w, xs, gate, ln_g, ln_b, mod_idx):
    t = xs.shape[0]
    tm = ROW_TILE
    return pl.pallas_call(
        _proj_ln_kernel,
        out_shape=jax.ShapeDtypeStruct((t, D), F32),
        grid=(t // tm,),
        in_specs=[
            pl.BlockSpec((tm, D), lambda i: (i, 0)),
            pl.BlockSpec((D, D), lambda i: (0, 0)),
            pl.BlockSpec((tm, D), lambda i: (i, 0)),
            pl.BlockSpec((None, 1, D), mod_idx),
            pl.BlockSpec((1, D), lambda i: (0, 0)),
            pl.BlockSpec((1, D), lambda i: (0, 0)),
        ],
        out_specs=pl.BlockSpec((tm, D), lambda i: (i, 0)),
        compiler_params=_cparams("arbitrary"),
        name="proj_ln",
    )(a, w, xs, gate, ln_g.reshape(1, D), ln_b.reshape(1, D))


def _fn_in_kernel(x_ref, sc_ref, sh_ref, w_ref, g_ref, cs_ref, a_ref, b_ref):
    h = x_ref[...] * (1.0 + sc_ref[...]) + sh_ref[...]
    u = jnp.dot(h.astype(BF16), w_ref[...], preferred_element_type=F32)
    gain = g_ref[...]
    cs = cs_ref[...]
    for g in range(FN_GROUPS):
        cols = slice(g * FN_GROUP_DIM, (g + 1) * FN_GROUP_DIM)
        ug = u[:, cols]
        r = ug * lax.rsqrt(jnp.mean(ug * ug, axis=-1, keepdims=True) + RMS_EPS) * gain[:, cols]
        ab = jnp.dot(r.astype(BF16), cs, preferred_element_type=F32)
        a_ref[:, cols] = ab[:, :FN_GROUP_DIM].astype(BF16)
        b_ref[:, cols] = ab[:, FN_GROUP_DIM:].astype(BF16)


def _fn_in_call(xs, sc, sh, w, norm_g, cs, mod_idx):
    t = xs.shape[0]
    tm = ROW_TILE
    out = jax.ShapeDtypeStruct((t, D), BF16)
    return pl.pallas_call(
        _fn_in_kernel,
        out_shape=(out, out),
        grid=(t // tm,),
        in_specs=[
            pl.BlockSpec((tm, D), lambda i: (i, 0)),
            pl.BlockSpec((None, 1, D), mod_idx),
            pl.BlockSpec((None, 1, D), mod_idx),
            pl.BlockSpec((D, D), lambda i: (0, 0)),
            pl.BlockSpec((1, D), lambda i: (0, 0)),
            pl.BlockSpec((FN_GROUP_DIM, 2 * FN_GROUP_DIM), lambda i: (0, 0)),
        ],
        out_specs=(pl.BlockSpec((tm, D), lambda i: (i, 0)),
                   pl.BlockSpec((tm, D), lambda i: (i, 0))),
        compiler_params=_cparams("arbitrary"),
        name="fn_in",
    )(xs, sc, sh, w, norm_g.reshape(1, D), cs)


def _dft_tables(n):
    idx = jnp.arange(n, dtype=jnp.int32)
    prod = (idx[:, None] * idx[None, :]) % n
    ang = prod.astype(F32) * (2.0 * np.pi / n)
    return jnp.cos(ang), jnp.sin(ang)


def _dft_kernel(c_ref, s_ref, a0_ref, a1_ref, b0_ref, b1_ref, o_ref, acc_ref, *, scale):
    kj = pl.program_id(1)

    @pl.when(kj == 0)
    def _():
        acc_ref[...] = jnp.zeros(acc_ref.shape, F32)

    c = c_ref[...]
    s = s_ref[...]
    acc_ref[0] += (jnp.dot(c, a0_ref[...], preferred_element_type=F32)
                   - jnp.dot(s, b0_ref[...], preferred_element_type=F32))
    acc_ref[1] += (jnp.dot(c, a1_ref[...], preferred_element_type=F32)
                   - jnp.dot(s, b1_ref[...], preferred_element_type=F32))

    @pl.when(kj == pl.num_programs(1) - 1)
    def _():
        o_ref[...] = (acc_ref[...] * scale).astype(o_ref.dtype)


def _dft_call(ct, st, a, bp, *, seq, row0, tile):
    nt = seq // tile
    blk0 = row0 // tile
    kern = functools.partial(_dft_kernel, scale=float((seq * FN_GROUP_DIM) ** -0.5))
    seq_spec0 = pl.BlockSpec((tile, D), lambda i, j: (blk0 + j, 0))
    seq_spec1 = pl.BlockSpec((tile, D), lambda i, j: (blk0 + nt + j, 0))
    return pl.pallas_call(
        kern,
        out_shape=jax.ShapeDtypeStruct((2, seq, D), BF16),
        grid=(nt, nt),
        in_specs=[
            pl.BlockSpec((tile, tile), lambda i, j: (i, j)),
            pl.BlockSpec((tile, tile), lambda i, j: (i, j)),
            seq_spec0, seq_spec1, seq_spec0, seq_spec1,
        ],
        out_specs=pl.BlockSpec((2, tile, D), lambda i, j: (0, i, 0)),
        scratch_shapes=[pltpu.VMEM((2, tile, D), F32)],
        compiler_params=_cparams("arbitrary", "arbitrary"),
        name="seq_dft",
    )(ct, st, a, a, bp, bp)


def _router_kernel(x_ref, sc_ref, sh_ref, w_ref, b_ref, h_ref, meta_ref, gate_ref, cnt_ref,
                   carry_ref):
    i = pl.program_id(0)

    @pl.when(i == 0)
    def _():
        carry_ref[...] = jnp.zeros(carry_ref.shape, F32)

    h = x_ref[...] * (1.0 + sc_ref[...]) + sh_ref[...]
    h_ref[...] = h
    logits = jnp.dot(h, w_ref[...], precision=HIGHEST, preferred_element_type=F32) + b_ref[...]
    tm = logits.shape[0]
    lane = lax.broadcasted_iota(jnp.int32, logits.shape, 1).astype(F32)
    wide = lax.broadcasted_iota(jnp.int32, (tm, LANES), 1)

    work = logits
    onehot = jnp.zeros(logits.shape, F32)
    picks, vals = [], []
    for _ in range(TOP_K):
        mx = jnp.max(work, axis=1, keepdims=True)
        idx = jnp.min(jnp.where(work == mx, lane, float(N_EXPERTS)), axis=1, keepdims=True)
        sel = lane == idx
        onehot = onehot + sel.astype(F32)
        work = jnp.where(sel, -jnp.inf, work)
        picks.append(idx)
        vals.append(mx)

    exps = [jnp.exp(v - vals[0]) for v in vals]
    denom = exps[0] + exps[1] + exps[2] + exps[3]

    r_i = lax.broadcasted_iota(jnp.int32, (tm, tm), 0)
    c_i = lax.broadcasted_iota(jnp.int32, (tm, tm), 1)
    tri = (c_i < r_i).astype(BF16)
    prefix = jnp.dot(tri, onehot.astype(BF16), preferred_element_type=F32) + carry_ref[0:1, 0:N_EXPERTS]

    meta = jnp.zeros((tm, LANES), jnp.int32)
    gates = jnp.zeros((tm, LANES), F32)
    for k in range(TOP_K):
        rank = jnp.sum(jnp.where(lane == picks[k], prefix, 0.0), axis=1, keepdims=True)
        meta = jnp.where(wide == k, picks[k].astype(jnp.int32), meta)
        meta = jnp.where(wide == TOP_K + k, rank.astype(jnp.int32), meta)
        gates = jnp.where(wide == k, exps[k] / denom, gates)
    meta_ref[...] = meta
    gate_ref[...] = gates

    total = carry_ref[0:1, 0:N_EXPERTS] + jnp.sum(onehot, axis=0, keepdims=True)
    carry_ref[0:1, 0:N_EXPERTS] = total
    cnt_ref[...] = jnp.zeros(cnt_ref.shape, jnp.int32)
    cnt_ref[0:1, 0:N_EXPERTS] = total.astype(jnp.int32)


def _router_call(xs, sc, sh, w, b, mod_idx):
    t = xs.shape[0]
    tm = ROW_TILE
    return pl.pallas_call(
        _router_kernel,
        out_shape=(jax.ShapeDtypeStruct((t, D), F32),
                   jax.ShapeDtypeStruct((t, LANES), jnp.int32),
                   jax.ShapeDtypeStruct((t, LANES), F32),
                   jax.ShapeDtypeStruct((8, LANES), jnp.int32)),
        grid=(t // tm,),
        in_specs=[
            pl.BlockSpec((tm, D), lambda i: (i, 0)),
            pl.BlockSpec((None, 1, D), mod_idx),
            pl.BlockSpec((None, 1, D), mod_idx),
            pl.BlockSpec((D, N_EXPERTS), lambda i: (0, 0)),
            pl.BlockSpec((1, N_EXPERTS), lambda i: (0, 0)),
        ],
        out_specs=(pl.BlockSpec((tm, D), lambda i: (i, 0)),
                   pl.BlockSpec((tm, LANES), lambda i: (i, 0)),
                   pl.BlockSpec((tm, LANES), lambda i: (i, 0)),
                   pl.BlockSpec((8, LANES), lambda i: (0, 0))),
        scratch_shapes=[pltpu.VMEM((8, LANES), F32)],
        compiler_params=_cparams("arbitrary"),
        name="router",
    )(xs, sc, sh, w, b.reshape(1, N_EXPERTS))


def _dispatch_kernel(dest_ref, h_ref, init_ref, xs_ref, sem):
    del init_ref
    tm = h_ref.shape[0]
    t0 = pl.program_id(0) * tm

    def row_copy(r, d):
        return pltpu.make_async_copy(h_ref.at[pl.ds(r, 1)], xs_ref.at[pl.ds(d, 1)], sem)

    def issue(r, carry):
        base = (t0 + r) * TOP_K
        for k in range(TOP_K):
            row_copy(r, dest_ref[base + k]).start()
        return carry
    lax.fori_loop(0, tm, issue, 0)

    def drain(r, carry):
        row_copy(0, 0).wait()
        return carry
    lax.fori_loop(0, tm * TOP_K, drain, 0)


def _dispatch_call(dest, h, cap):
    t = h.shape[0]
    tm = ROW_TILE
    return pl.pallas_call(
        _dispatch_kernel,
        out_shape=jax.ShapeDtypeStruct((cap, D), F32),
        grid_spec=pltpu.PrefetchScalarGridSpec(
            num_scalar_prefetch=1,
            grid=(t // tm,),
            in_specs=[pl.BlockSpec((tm, D), lambda i, d: (i, 0)),
                      pl.BlockSpec(memory_space=pl.ANY)],
            out_specs=pl.BlockSpec(memory_space=pl.ANY),
            scratch_shapes=[pltpu.SemaphoreType.DMA],
        ),
        input_output_aliases={2: 0},
        compiler_params=_cparams("arbitrary"),
        name="moe_dispatch",
    )(dest, h, jnp.zeros((cap, D), F32))


def _expert_kernel(blk_e_ref, n_used_ref, x_ref, wgu_ref, bgu_ref, wd_ref, bd_ref, y_ref):
    del blk_e_ref
    i = pl.program_id(0)
    f = wd_ref.shape[0]

    @pl.when(i < n_used_ref[0])
    def _():
        xb = x_ref[...].astype(BF16)
        gu = jnp.dot(xb, wgu_ref[...], preferred_element_type=F32) + bgu_ref[...]
        gate = jnp.minimum(gu[:, :f], SWIGLU_LIMIT)
        up = jnp.clip(gu[:, f:], -SWIGLU_LIMIT, SWIGLU_LIMIT)
        act = gate / (1.0 + jnp.exp(-SWIGLU_ALPHA * gate)) * (up + 1.0)
        y_ref[...] = jnp.dot(act.astype(BF16), wd_ref[...], preferred_element_type=F32) + bd_ref[...]

    @pl.when(i >= n_used_ref[0])
    def _():
        y_ref[...] = jnp.zeros(y_ref.shape, F32)


def _expert_call(blk_e, n_used, xs, w_gu, b_gu, w_down, b_down):
    cap = xs.shape[0]
    blk = EXPERT_BLOCK
    e, _, f2 = w_gu.shape
    f = w_down.shape[1]
    return pl.pallas_call(
        _expert_kernel,
        out_shape=jax.ShapeDtypeStruct((cap, D), F32),
        grid_spec=pltpu.PrefetchScalarGridSpec(
            num_scalar_prefetch=2,
            grid=(cap // blk,),
            in_specs=[
                pl.BlockSpec((blk, D), lambda i, be, nu: (i, 0)),
                pl.BlockSpec((None, D, f2), lambda i, be, nu: (be[i], 0, 0)),
                pl.BlockSpec((None, 1, f2), lambda i, be, nu: (be[i], 0, 0)),
                pl.BlockSpec((None, f, D), lambda i, be, nu: (be[i], 0, 0)),
                pl.BlockSpec((None, 1, D), lambda i, be, nu: (be[i], 0, 0)),
            ],
            out_specs=pl.BlockSpec((blk, D), lambda i, be, nu: (i, 0)),
        ),
        compiler_params=_cparams("arbitrary"),
        name="moe_experts",
    )(blk_e, n_used, xs, w_gu, b_gu.reshape(e, 1, f2), w_down, b_down.reshape(e, 1, D))


def _combine_kernel(dest_ref, y_ref, gates_ref, x_ref, g2_ref, lng_ref, lnb_ref, o_ref, buf, sem):
    i = pl.program_id(0)
    tm = x_ref.shape[0]

    def row_copy(r, k, d):
        return pltpu.make_async_copy(y_ref.at[pl.ds(d, 1)], buf.at[k, pl.ds(r, 1)], sem)

    def issue(r, carry):
        base = (i * tm + r) * TOP_K
        for k in range(TOP_K):
            row_copy(r, k, dest_ref[base + k]).start()
        return carry
    lax.fori_loop(0, tm, issue, 0)

    def drain(r, carry):
        row_copy(0, 0, 0).wait()
        return carry
    lax.fori_loop(0, tm * TOP_K, drain, 0)

    gates = gates_ref[...]
    y = gates[:, 0:1] * buf[0]
    for k in range(1, TOP_K):
        y = y + gates[:, k:k + 1] * buf[k]
    z = DEEPNORM_ALPHA * x_ref[...] + g2_ref[...] * y
    o_ref[...] = _layer_norm_rows(z, lng_ref[...], lnb_ref[...])


def _combine_call(dest, y, gates, xs, g2, ln_g, ln_b, lat_tiles, batch):
    t = xs.shape[0]
    tm = ROW_TILE

    def mod_idx(i, d):
        return (jnp.minimum(i // lat_tiles, batch), 0, 0)

    return pl.pallas_call(
        _combine_kernel,
        out_shape=jax.ShapeDtypeStruct((t, D), F32),
        grid_spec=pltpu.PrefetchScalarGridSpec(
            num_scalar_prefetch=1,
            grid=(t // tm,),
            in_specs=[
                pl.BlockSpec(memory_space=pl.ANY),
                pl.BlockSpec((tm, LANES), lambda i, d: (i, 0)),
                pl.BlockSpec((tm, D), lambda i, d: (i, 0)),
                pl.BlockSpec((None, 1, D), mod_idx),
                pl.BlockSpec((1, D), lambda i, d: (0, 0)),
                pl.BlockSpec((1, D), lambda i, d: (0, 0)),
            ],
            out_specs=pl.BlockSpec((tm, D), lambda i, d: (i, 0)),
            scratch_shapes=[pltpu.VMEM((TOP_K, tm, D), F32), pltpu.SemaphoreType.DMA],
        ),
        compiler_params=_cparams("arbitrary"),
        name="moe_combine_ln",
    )(dest, y, gates, xs, g2, ln_g.reshape(1, D), ln_b.reshape(1, D))


def _moe_plan(meta, counts, cap):
    blk = EXPERT_BLOCK
    ids = meta[:, :TOP_K]
    rank = meta[:, TOP_K:2 * TOP_K]
    cnt = counts[0, :N_EXPERTS]
    padded = (cnt + blk - 1) // blk * blk
    pend = jnp.cumsum(padded)
    pstart = pend - padded
    dest = (pstart[ids] + rank).reshape(-1).astype(jnp.int32)
    n_blocks = cap // blk
    blk_e = jnp.minimum(jnp.searchsorted(pend, jnp.arange(n_blocks) * blk, side="right"),
                        N_EXPERTS - 1).astype(jnp.int32)
    n_used = (pend[-1:] // blk).astype(jnp.int32)
    return dest, blk_e, n_used


def kernel(x, c, ctx, c_ctx, w_mod, b_mod, ln_g, ln_b, attn_w_qkv, attn_lambda, attn_subln_g,
           attn_w_o, fn_w_in, fn_norm_g, fn_w_out, moe_w_router, moe_b_router, moe_w_gu,
           moe_b_gu, moe_w_down, moe_b_down):
    batch, n_lat, d = x.shape
    n_ctx = ctx.shape[1]
    assert d == D and batch == 2 and n_lat % DFT_TILE == 0 and n_ctx == ROW_TILE
    t_lat = batch * n_lat
    t_all = t_lat + batch * n_ctx
    lat_tiles = n_lat // ROW_TILE
    last_ctx_layer = max(i for i in range(DEPTH) if i % N_MIXERS == 0)

    def mod_idx(i):
        return (jnp.minimum(i // lat_tiles, batch), 0, 0)

    xs = jnp.concatenate([x.reshape(t_lat, D), ctx.reshape(batch * n_ctx, D)], axis=0)
    cc = jnp.concatenate([c, c_ctx[None, :], jnp.zeros((8 - batch - 1, D), F32)], axis=0)
    mods = _mod_call(cc, w_mod, b_mod)

    cos, sin = _rope_tables(n_lat, ROW_TILE)
    cs = jnp.concatenate(_dft_tables(FN_GROUP_DIM), axis=1).astype(BF16)
    ct_lat, st_lat = [m.astype(BF16) for m in _dft_tables(n_lat)]
    ct_ctx, st_ctx = [m.astype(BF16) for m in _dft_tables(n_ctx)]

    n_assign = t_all * TOP_K
    cap = -(-n_assign // EXPERT_BLOCK) * EXPERT_BLOCK + N_EXPERTS * EXPERT_BLOCK

    for i in range(DEPTH):
        j = i // N_MIXERS
        ctx_full = i < last_ctx_layer

        def mvec(k, i=i):
            return mods[i, :batch + 1, k * D:(k + 1) * D].reshape(batch + 1, 1, D)

        sh1, sc1, g1, sh2, sc2, g2 = [mvec(k) for k in range(6)]

        if i % N_MIXERS == 0:
            lambda_init = 0.8 - 0.6 * float(np.exp(-0.3 * i))
            q, k_all, vt_all = _qkv_call(xs, sc1, sh1, attn_w_qkv[j].astype(BF16), cos, sin,
                                         n_lat, n_ctx, batch)
            o_lat = _attn_call(q, k_all, vt_all, attn_lambda[j], attn_subln_g[j], lambda_init,
                               batch=batch, q_rows=n_lat, q_row0=0, tq=ATTN_Q_TILE,
                               chunk0=0, n_chunks=lat_tiles + 1)
            if ctx_full:
                o_ctx = _attn_call(q, k_all, vt_all, attn_lambda[j], attn_subln_g[j], lambda_init,
                                   batch=batch, q_rows=n_ctx, q_row0=t_lat, tq=n_ctx,
                                   chunk0=lat_tiles, n_chunks=1)
            else:
                o_ctx = jnp.zeros((batch * n_ctx, D), BF16)
            branch = jnp.concatenate([o_lat, o_ctx], axis=0)
            w_out = attn_w_o[j]
        else:
            a, bp = _fn_in_call(xs, sc1, sh1, fn_w_in[j].astype(BF16), fn_norm_g[j], cs, mod_idx)
            f_lat = _dft_call(ct_lat, st_lat, a, bp, seq=n_lat, row0=0, tile=DFT_TILE)
            if ctx_full:
                f_ctx = _dft_call(ct_ctx, st_ctx, a, bp, seq=n_ctx, row0=t_lat, tile=n_ctx)
            else:
                f_ctx = jnp.zeros((batch, n_ctx, D), BF16)
            branch = jnp.concatenate([f_lat.reshape(t_lat, D), f_ctx.reshape(batch * n_ctx, D)],
                                     axis=0)
            w_out = fn_w_out[j]
        xs = _proj_ln_call(branch, w_out.astype(BF16), xs, g1, ln_g[i, 0], ln_b[i, 0], mod_idx)

        h2, meta, gates, counts = _router_call(xs, sc2, sh2, moe_w_router[i], moe_b_router[i],
                                               mod_idx)
        dest, blk_e, n_used = _moe_plan(meta, counts, cap)
        xd = _dispatch_call(dest, h2, cap)
        y = _expert_call(blk_e, n_used, xd, moe_w_gu[i].astype(BF16), moe_b_gu[i],
                         moe_w_down[i].astype(BF16), moe_b_down[i])
        xs = _combine_call(dest, y, gates, xs, g2, ln_g[i, 1], ln_b[i, 1], lat_tiles, batch)

    return xs[:t_lat].reshape(batch, n_lat, D)
```

```python
import functools

import jax
import jax.numpy as jnp
from jax import lax
import numpy as np
from jax.experimental import pallas as pl
from jax.experimental.pallas import tpu as pltpu

F32 = jnp.float32
BF16 = jnp.bfloat16
HIGHEST = lax.Precision.HIGHEST

D = 1024
GRID_W = 64
DEPTH = 4
N_MIXERS = 2
HEAD_DIM = 64
HEADS = 8
V_DIM = 128
ROPE_THETA = 10000.0
ROPE_FREQS = HEAD_DIM // 4
FN_GROUPS = 8
FN_GROUP_DIM = D // FN_GROUPS
N_EXPERTS = 32
TOP_K = 4
SWIGLU_LIMIT = 7.0
SWIGLU_ALPHA = 1.702
LN_EPS = 1e-5
RMS_EPS = 1e-5
DEEPNORM_ALPHA = (2 * DEPTH) ** 0.25

LANES = 128
ROW_TILE = 256
ATTN_Q_TILE = 512
ATTN_K_TILE = 512
SUM_ROWS = 16
EXPERT_BLOCK = 512
Q_SCALE = HEAD_DIM ** -0.5 * float(np.log2(np.e))
DFT_TILE = 1024
DMA_UNROLL = 8
DMA_WAIT_UNROLL = 32
SLOT_ROW_BITS = 17
SLOT_ROW_MASK = (1 << SLOT_ROW_BITS) - 1
VMEM_LIMIT = 56 * 1024 * 1024


def _cparams(*sem):
    return pltpu.CompilerParams(dimension_semantics=sem, vmem_limit_bytes=VMEM_LIMIT)


def _loop_unrolled(n, unroll, fn):
    assert n % unroll == 0

    def body(g, carry):
        for u in range(unroll):
            fn(g * unroll + u)
        return carry
    lax.fori_loop(0, n // unroll, body, 0)


def _layer_norm_rows(z, g, b):
    mu = jnp.mean(z, axis=-1, keepdims=True)
    zc = z - mu
    var = jnp.mean(zc * zc, axis=-1, keepdims=True)
    return zc * lax.rsqrt(var + LN_EPS) * g + b


def _mod_kernel(c_ref, w_ref, b_ref, o_ref):
    c = c_ref[...]
    s = c / (1.0 + jnp.exp(-c))
    o_ref[...] = jnp.dot(s, w_ref[...], precision=HIGHEST, preferred_element_type=F32) + b_ref[...]


def _mod_call(cc, w_mod, b_mod):
    depth, _, width = w_mod.shape
    tn = 1536
    return pl.pallas_call(
        _mod_kernel,
        out_shape=jax.ShapeDtypeStruct((depth, 8, width), F32),
        grid=(depth, width // tn),
        in_specs=[
            pl.BlockSpec((8, D), lambda l, j: (0, 0)),
            pl.BlockSpec((None, D, tn), lambda l, j: (l, 0, j)),
            pl.BlockSpec((None, 1, tn), lambda l, j: (l, 0, j)),
        ],
        out_specs=pl.BlockSpec((None, 8, tn), lambda l, j: (l, 0, j)),
        compiler_params=_cparams("arbitrary", "arbitrary"),
        name="mod",
    )(cc, w_mod, b_mod.reshape(depth, 1, width))


def _qkv_kernel(x_ref, sc_ref, sh_ref, w_ref, cos_ref, sin_ref, q_ref, k_ref, vt_ref):
    h = x_ref[...] * (1.0 + sc_ref[...]) + sh_ref[...]
    y = jnp.dot(h.astype(BF16), w_ref[...], preferred_element_type=F32)
    cos = cos_ref[...]
    sin = sin_ref[...]
    lane = lax.broadcasted_iota(jnp.int32, cos.shape, 1)
    first_half = (lane % (2 * ROPE_FREQS)) < ROPE_FREQS

    def rope(t):
        partner = jnp.where(first_half,
                            pltpu.roll(t, LANES - ROPE_FREQS, axis=1),
                            pltpu.roll(t, ROPE_FREQS, axis=1))
        return t * cos + partner * sin

    for j in range(D // LANES):
        cols = slice(j * LANES, (j + 1) * LANES)
        q_ref[:, cols] = (rope(y[:, cols]) * Q_SCALE).astype(BF16)
        k_ref[:, cols] = rope(y[:, D + j * LANES:D + (j + 1) * LANES]).astype(BF16)
    vt_ref[...] = y[:, 2 * D:].T.astype(BF16)


def _qkv_call(xs, sc, sh, w, cos, sin, n_lat, n_ctx, batch):
    t = xs.shape[0]
    tm = ROW_TILE
    lat_tiles = n_lat // tm
    all_lat = batch * lat_tiles

    def mod_idx(i):
        return (jnp.minimum(i // lat_tiles, batch), 0, 0)

    def rope_idx(i):
        return (jnp.where(i < all_lat, i % lat_tiles, lat_tiles), 0)

    def kv_idx(i):
        return (jnp.where(i < all_lat, i // lat_tiles, i - all_lat),
                jnp.where(i < all_lat, i % lat_tiles, lat_tiles), 0)

    def vt_idx(i):
        return kv_idx(i)[:2] + (0, 0)

    return pl.pallas_call(
        _qkv_kernel,
        out_shape=(jax.ShapeDtypeStruct((t, D), BF16),
                   jax.ShapeDtypeStruct((batch, n_lat + n_ctx, D), BF16),
                   jax.ShapeDtypeStruct((batch, lat_tiles + 1, D, tm), BF16)),
        grid=(t // tm,),
        in_specs=[
            pl.BlockSpec((tm, D), lambda i: (i, 0)),
            pl.BlockSpec((None, 1, D), mod_idx),
            pl.BlockSpec((None, 1, D), mod_idx),
            pl.BlockSpec((D, 3 * D), lambda i: (0, 0)),
            pl.BlockSpec((tm, LANES), rope_idx),
            pl.BlockSpec((tm, LANES), rope_idx),
        ],
        out_specs=(
            pl.BlockSpec((tm, D), lambda i: (i, 0)),
            pl.BlockSpec((None, tm, D), kv_idx),
            pl.BlockSpec((None, None, D, tm), vt_idx),
        ),
        compiler_params=_cparams("arbitrary"),
        name="qkv_rope",
    )(xs, sc, sh, w, cos, sin)


def _rope_tables(n_lat, pad_rows):
    rows = n_lat // GRID_W
    row_ids = jnp.repeat(jnp.arange(rows), GRID_W).astype(F32)
    col_ids = jnp.tile(jnp.arange(GRID_W), rows).astype(F32)
    inv_freq = ROPE_THETA ** (-jnp.arange(ROPE_FREQS, dtype=F32) / ROPE_FREQS)
    ar = row_ids[:, None] * inv_freq
    ac = col_ids[:, None] * inv_freq
    cos = jnp.concatenate([jnp.cos(ar), jnp.cos(ar), jnp.cos(ac), jnp.cos(ac)], axis=1)
    sin = jnp.concatenate([-jnp.sin(ar), jnp.sin(ar), -jnp.sin(ac), jnp.sin(ac)], axis=1)
    reps = LANES // HEAD_DIM
    cos = jnp.tile(cos, (1, reps))
    sin = jnp.tile(sin, (1, reps))
    cos = jnp.concatenate([cos, jnp.ones((pad_rows, LANES), F32)], axis=0)
    sin = jnp.concatenate([sin, jnp.zeros((pad_rows, LANES), F32)], axis=0)
    return cos, sin


def _attn_kernel(lam_ref, g_ref, q_ref, k_ref, vt_ref, o_ref, qq_ref, s_ref, m_ref, acc_ref, *,
                 n_chunks, chunks_per_step, lambda_init):
    tq = q_ref.shape[0]
    kc = vt_ref.shape[2]
    qt = q_ref[...].astype(F32).T
    sub = lax.broadcasted_iota(jnp.int32, qt.shape, 0)
    qq_ref[...] = jnp.concatenate([jnp.where(sub < HEAD_DIM, qt, 0.0),
                                   jnp.where(sub >= HEAD_DIM, qt, 0.0)], axis=1).astype(BF16)
    m_ref[...] = jnp.full(m_ref.shape, -jnp.inf, F32)
    acc_ref[...] = jnp.zeros(acc_ref.shape, F32)
    ones_rows = jnp.ones((SUM_ROWS, kc), BF16)

    def scores(slot, c0, nc):
        start = c0 * kc
        if not isinstance(start, int):
            start = pl.multiple_of(start, kc)
        kj = k_ref[pl.ds(start, nc * kc), :]
        s_ref[slot, 0:nc * kc, :] = jnp.dot(kj, qq_ref[...], preferred_element_type=F32)

    def accumulate(slot, c0, nc):
        s = s_ref[slot, 0:nc * kc, :]
        m_old = m_ref[...]
        m_new = jnp.maximum(m_old, jnp.max(s, axis=0, keepdims=True))
        alpha = jnp.exp2(m_old - m_new)
        pb = jnp.exp2(s - m_new).astype(BF16)
        pv = None
        for c in range(nc):
            v_ext = jnp.concatenate([vt_ref[c0 + c], ones_rows], axis=0)
            part = jnp.dot(v_ext, pb[c * kc:(c + 1) * kc], preferred_element_type=F32)
            pv = part if pv is None else pv + part
        acc_ref[...] = alpha * acc_ref[...] + pv
        m_ref[...] = m_new

    cps = chunks_per_step
    n_full = n_chunks // cps
    steps = [(i * cps, cps) for i in range(n_full)]
    if n_chunks % cps:
        steps.append((n_full * cps, n_chunks % cps))
    n_pairs = max(n_full - 1, 0) // 2
    scores(0, *steps[0])
    if n_pairs > 0:
        def body(jj, carry):
            c0 = jj * (2 * cps)
            scores(1, c0 + cps, cps)
            accumulate(0, c0, cps)
            scores(0, c0 + 2 * cps, cps)
            accumulate(1, c0 + cps, cps)
            return carry
        lax.fori_loop(0, n_pairs, body, 0)
    slot = 0
    for idx in range(2 * n_pairs, len(steps)):
        if idx + 1 < len(steps):
            scores(1 - slot, *steps[idx + 1])
        accumulate(slot, *steps[idx])
        slot = 1 - slot

    lam = lam_ref[...]
    lam_val = (jnp.exp(jnp.sum(lam[0:1] * lam[1:2], axis=1, keepdims=True))
               - jnp.exp(jnp.sum(lam[2:3] * lam[3:4], axis=1, keepdims=True)) + lambda_init)
    o_all = acc_ref[0:V_DIM, :] / acc_ref[V_DIM:V_DIM + 1, :]
    o = (o_all[:, :tq] - lam_val * o_all[:, tq:]).T
    o = o * lax.rsqrt(jnp.mean(o * o, axis=-1, keepdims=True) + RMS_EPS)
    o_ref[...] = (o * g_ref[...] * (1.0 - lambda_init)).astype(o_ref.dtype)


def _attn_call(q, k_all, vt_all, lam, subln_g, lambda_init, *, batch, q_rows, q_row0, tq,
               chunk0, n_chunks):
    nq = q_rows // tq
    q_blk0 = q_row0 // tq
    kc = vt_all.shape[3]
    key_blk = chunk0 // n_chunks
    kern = functools.partial(_attn_kernel, n_chunks=n_chunks,
                             chunks_per_step=ATTN_K_TILE // kc, lambda_init=lambda_init)
    return pl.pallas_call(
        kern,
        out_shape=jax.ShapeDtypeStruct((batch * q_rows, D), BF16),
        grid=(batch, HEADS, nq),
        in_specs=[
            pl.BlockSpec((4, HEAD_DIM), lambda b, h, i: (0, 0)),
            pl.BlockSpec((1, V_DIM), lambda b, h, i: (0, 0)),
            pl.BlockSpec((tq, V_DIM), lambda b, h, i: (q_blk0 + b * nq + i, h)),
            pl.BlockSpec((None, n_chunks * kc, V_DIM), lambda b, h, i: (b, key_blk, h)),
            pl.BlockSpec((None, n_chunks, V_DIM, kc), lambda b, h, i: (b, key_blk, h, 0)),
        ],
        out_specs=pl.BlockSpec((tq, V_DIM), lambda b, h, i: (b * nq + i, h)),
        scratch_shapes=[
            pltpu.VMEM((V_DIM, 2 * tq), BF16),
            pltpu.VMEM((2, ATTN_K_TILE, 2 * tq), F32),
            pltpu.VMEM((1, 2 * tq), F32),
            pltpu.VMEM((V_DIM + SUM_ROWS, 2 * tq), F32),
        ],
        compiler_params=_cparams("arbitrary", "arbitrary", "arbitrary"),
        name="diff_attn",
    )(lam, subln_g.reshape(1, V_DIM), q, k_all, vt_all)


def _proj_ln_kernel(a_ref, w_ref, x_ref, gate_ref, lng_ref, lnb_ref, o_ref):
    y = jnp.dot(a_ref[...], w_ref[...], preferred_element_type=F32)
    z = DEEPNORM_ALPHA * x_ref[...] + gate_ref[...] * y
    o_ref[...] = _layer_norm_rows(z, lng_ref[...], lnb_ref[...])


def _proj_ln_call(a, w, xs, gate, ln_g, ln_b, mod_idx):
    t = xs.shape[0]
    tm = ROW_TILE
    return pl.pallas_call(
        _proj_ln_kernel,
        out_shape=jax.ShapeDtypeStruct((t, D), F32),
        grid=(t // tm,),
        in_specs=[
            pl.BlockSpec((tm, D), lambda i: (i, 0)),
            pl.BlockSpec((D, D), lambda i: (0, 0)),
            pl.BlockSpec((tm, D), lambda i: (i, 0)),
            pl.BlockSpec((None, 1, D), mod_idx),
            pl.BlockSpec((1, D), lambda i: (0, 0)),
            pl.BlockSpec((1, D), lambda i: (0, 0)),
        ],
        out_specs=pl.BlockSpec((tm, D), lambda i: (i, 0)),
        compiler_params=_cparams("arbitrary"),
        name="proj_ln",
    )(a, w, xs, gate, ln_g.reshape(1, D), ln_b.reshape(1, D))


def _fn_in_kernel(x_ref, sc_ref, sh_ref, w_ref, g_ref, cs_ref, a_ref, b_ref):
    h = x_ref[...] * (1.0 + sc_ref[...]) + sh_ref[...]
    u = jnp.dot(h.astype(BF16), w_ref[...], preferred_element_type=F32)
    gain = g_ref[...]
    cs = cs_ref[...]
    for g in range(FN_GROUPS):
        cols = slice(g * FN_GROUP_DIM, (g + 1) * FN_GROUP_DIM)
        ug = u[:, cols]
        r = ug * lax.rsqrt(jnp.mean(ug * ug, axis=-1, keepdims=True) + RMS_EPS) * gain[:, cols]
        ab = jnp.dot(r.astype(BF16), cs, preferred_element_type=F32)
        a_ref[:, cols] = ab[:, :FN_GROUP_DIM].astype(BF16)
        b_ref[:, cols] = ab[:, FN_GROUP_DIM:].astype(BF16)


def _fn_in_call(xs, sc, sh, w, norm_g, cs, mod_idx):
    t = xs.shape[0]
    tm = ROW_TILE
    out = jax.ShapeDtypeStruct((t, D), BF16)
    return pl.pallas_call(
        _fn_in_kernel,
        out_shape=(out, out),
        grid=(t // tm,),
        in_specs=[
            pl.BlockSpec((tm, D), lambda i: (i, 0)),
            pl.BlockSpec((None, 1, D), mod_idx),
            pl.BlockSpec((None, 1, D), mod_idx),
            pl.BlockSpec((D, D), lambda i: (0, 0)),
            pl.BlockSpec((1, D), lambda i: (0, 0)),
            pl.BlockSpec((FN_GROUP_DIM, 2 * FN_GROUP_DIM), lambda i: (0, 0)),
        ],
        out_specs=(pl.BlockSpec((tm, D), lambda i: (i, 0)),
                   pl.BlockSpec((tm, D), lambda i: (i, 0))),
        compiler_params=_cparams("arbitrary"),
        name="fn_in",
    )(xs, sc, sh, w, norm_g.reshape(1, D), cs)


def _dft_tables(n):
    idx = jnp.arange(n, dtype=jnp.int32)
    prod = (idx[:, None] * idx[None, :]) % n
    ang = prod.astype(F32) * (2.0 * np.pi / n)
    return jnp.cos(ang), jnp.sin(ang)


def _dft_kernel(c_ref, s_ref, a0_ref, a1_ref, b0_ref, b1_ref, o_ref, acc_ref, *, scale):
    kj = pl.program_id(1)

    @pl.when(kj == 0)
    def _():
        acc_ref[...] = jnp.zeros(acc_ref.shape, F32)

    c = c_ref[...]
    s = s_ref[...]
    acc_ref[0] += (jnp.dot(c, a0_ref[...], preferred_element_type=F32)
                   - jnp.dot(s, b0_ref[...], preferred_element_type=F32))
    acc_ref[1] += (jnp.dot(c, a1_ref[...], preferred_element_type=F32)
                   - jnp.dot(s, b1_ref[...], preferred_element_type=F32))

    @pl.when(kj == pl.num_programs(1) - 1)
    def _():
        o_ref[...] = (acc_ref[...] * scale).astype(o_ref.dtype)


def _dft_call(ct, st, a, bp, *, seq, row0, tile):
    nt = seq // tile
    blk0 = row0 // tile
    kern = functools.partial(_dft_kernel, scale=float((seq * FN_GROUP_DIM) ** -0.5))
    seq_spec0 = pl.BlockSpec((tile, D), lambda i, j: (blk0 + j, 0))
    seq_spec1 = pl.BlockSpec((tile, D), lambda i, j: (blk0 + nt + j, 0))
    return pl.pallas_call(
        kern,
        out_shape=jax.ShapeDtypeStruct((2, seq, D), BF16),
        grid=(nt, nt),
        in_specs=[
            pl.BlockSpec((tile, tile), lambda i, j: (i, j)),
            pl.BlockSpec((tile, tile), lambda i, j: (i, j)),
            seq_spec0, seq_spec1, seq_spec0, seq_spec1,
        ],
        out_specs=pl.BlockSpec((2, tile, D), lambda i, j: (0, i, 0)),
        scratch_shapes=[pltpu.VMEM((2, tile, D), F32)],
        compiler_params=_cparams("arbitrary", "arbitrary"),
        name="seq_dft",
    )(ct, st, a, a, bp, bp)


def _router_kernel(x_ref, sc_ref, sh_ref, w_ref, b_ref, h_ref, meta_ref, gate_ref, cnt_ref,
                   carry_ref):
    i = pl.program_id(0)

    @pl.when(i == 0)
    def _():
        carry_ref[...] = jnp.zeros(carry_ref.shape, F32)

    h = x_ref[...] * (1.0 + sc_ref[...]) + sh_ref[...]
    h_ref[...] = h
    logits = jnp.dot(h, w_ref[...], precision=HIGHEST, preferred_element_type=F32) + b_ref[...]
    tm = logits.shape[0]
    lane = lax.broadcasted_iota(jnp.int32, logits.shape, 1).astype(F32)
    wide = lax.broadcasted_iota(jnp.int32, (tm, LANES), 1)

    work = logits
    onehot = jnp.zeros(logits.shape, F32)
    picks, vals = [], []
    for _ in range(TOP_K):
        mx = jnp.max(work, axis=1, keepdims=True)
        idx = jnp.min(jnp.where(work == mx, lane, float(N_EXPERTS)), axis=1, keepdims=True)
        sel = lane == idx
        onehot = onehot + sel.astype(F32)
        work = jnp.where(sel, -jnp.inf, work)
        picks.append(idx)
        vals.append(mx)

    exps = [jnp.exp(v - vals[0]) for v in vals]
    denom = exps[0] + exps[1] + exps[2] + exps[3]

    r_i = lax.broadcasted_iota(jnp.int32, (tm, tm), 0)
    c_i = lax.broadcasted_iota(jnp.int32, (tm, tm), 1)
    tri = (c_i < r_i).astype(BF16)
    prefix = jnp.dot(tri, onehot.astype(BF16), preferred_element_type=F32) + carry_ref[0:1, 0:N_EXPERTS]

    meta = jnp.zeros((tm, LANES), jnp.int32)
    gates = jnp.zeros((tm, LANES), F32)
    for k in range(TOP_K):
        rank = jnp.sum(jnp.where(lane == picks[k], prefix, 0.0), axis=1, keepdims=True)
        meta = jnp.where(wide == k, picks[k].astype(jnp.int32), meta)
        meta = jnp.where(wide == TOP_K + k, rank.astype(jnp.int32), meta)
        gates = jnp.where(wide == k, exps[k] / denom, gates)
    meta_ref[...] = meta
    gate_ref[...] = gates

    total = carry_ref[0:1, 0:N_EXPERTS] + jnp.sum(onehot, axis=0, keepdims=True)
    carry_ref[0:1, 0:N_EXPERTS] = total
    cnt_ref[...] = jnp.zeros(cnt_ref.shape, jnp.int32)
    cnt_ref[0:1, 0:N_EXPERTS] = total.astype(jnp.int32)


def _router_call(xs, sc, sh, w, b, mod_idx):
    t = xs.shape[0]
    tm = ROW_TILE
    return pl.pallas_call(
        _router_kernel,
        out_shape=(jax.ShapeDtypeStruct((t, D), F32),
                   jax.ShapeDtypeStruct((t, LANES), jnp.int32),
                   jax.ShapeDtypeStruct((t, LANES), F32),
                   jax.ShapeDtypeStruct((8, LANES), jnp.int32)),
        grid=(t // tm,),
        in_specs=[
            pl.BlockSpec((tm, D), lambda i: (i, 0)),
            pl.BlockSpec((None, 1, D), mod_idx),
            pl.BlockSpec((None, 1, D), mod_idx),
            pl.BlockSpec((D, N_EXPERTS), lambda i: (0, 0)),
            pl.BlockSpec((1, N_EXPERTS), lambda i: (0, 0)),
        ],
        out_specs=(pl.BlockSpec((tm, D), lambda i: (i, 0)),
                   pl.BlockSpec((tm, LANES), lambda i: (i, 0)),
                   pl.BlockSpec((tm, LANES), lambda i: (i, 0)),
                   pl.BlockSpec((8, LANES), lambda i: (0, 0))),
        scratch_shapes=[pltpu.VMEM((8, LANES), F32)],
        compiler_params=_cparams("arbitrary"),
        name="router",
    )(xs, sc, sh, w, b.reshape(1, N_EXPERTS))


def _expert_kernel(blk_e_ref, n_used_ref, words_ref, h_ref, wgu_ref, bgu_ref, wd_ref, bd_ref,
                   y_ref, wgu_bf, wd_bf, xbuf0, xbuf1, ybuf0, ybuf1, gsem, ssem, *, trash_row0):
    i = pl.program_id(0)
    blk = xbuf0.shape[0]
    f = wd_ref.shape[0]
    n_used = n_used_ref[0]
    xbufs = (xbuf0, xbuf1)
    ybufs = (ybuf0, ybuf1)

    def word(j, r):
        return words_ref[(j + 1) * blk + r]

    def gather_copy(tok, r, dst):
        return pltpu.make_async_copy(h_ref.at[pl.ds(tok, 1)], xbufs[dst].at[pl.ds(r, 1)],
                                     gsem.at[dst])

    def scatter_copy(row, r, src):
        return pltpu.make_async_copy(ybufs[src].at[pl.ds(r, 1)], y_ref.at[pl.ds(row, 1)],
                                     ssem.at[src])

    def start_gather(j, r, dst):
        gather_copy(lax.shift_right_logical(word(j, r), SLOT_ROW_BITS), r, dst).start()

    def start_scatter(j, r, src):
        scatter_copy(word(j, r) & SLOT_ROW_MASK, r, src).start()

    def wait_gathers(dst):
        _loop_unrolled(blk, DMA_WAIT_UNROLL, lambda r: gather_copy(0, 0, dst).wait())

    def wait_scatters(src):
        _loop_unrolled(blk, DMA_WAIT_UNROLL, lambda r: scatter_copy(0, 0, src).wait())

    @pl.when(i == 0)
    def _():
        ybuf0[...] = jnp.zeros(ybuf0.shape, F32)
        ybuf1[...] = jnp.zeros(ybuf1.shape, F32)
        init = pltpu.make_async_copy(ybuf0, y_ref.at[pl.ds(trash_row0, blk)], ssem.at[0])
        init.start()
        init.wait()
        _loop_unrolled(blk, DMA_UNROLL, lambda r: start_gather(0, r, 0))

    new_expert = jnp.logical_or(i == 0, blk_e_ref[i] != blk_e_ref[jnp.maximum(i - 1, 0)])

    @pl.when(jnp.logical_and(i < n_used, new_expert))
    def _():
        wgu_bf[...] = wgu_ref[...].astype(BF16)
        wd_bf[...] = wd_ref[...].astype(BF16)

    def block_step(slot):
        other = 1 - slot
        wait_gathers(slot)

        @pl.when(i >= 1)
        def _():
            wait_scatters(slot)

        nxt = jnp.minimum(i + 1, n_used - 1)

        def start_rows(part):
            for r in range(part * blk // 4, (part + 1) * blk // 4):
                start_gather(nxt, r, other)
                start_scatter(i - 1, r, other)

        start_rows(0)
        xb = xbufs[slot][...].astype(BF16)
        gu = jnp.dot(xb, wgu_bf[...], preferred_element_type=F32) + bgu_ref[...]
        start_rows(1)
        gate = jnp.minimum(gu[:, :f], SWIGLU_LIMIT)
        up = jnp.clip(gu[:, f:], -SWIGLU_LIMIT, SWIGLU_LIMIT)
        act = gate / (1.0 + jnp.exp(-SWIGLU_ALPHA * gate)) * (up + 1.0)
        start_rows(2)
        ybufs[slot][...] = (jnp.dot(act.astype(BF16), wd_bf[...], preferred_element_type=F32)
                            + bd_ref[...])
        start_rows(3)

        @pl.when(i == n_used - 1)
        def _():
            wait_gathers(other)
            wait_scatters(other)
            _loop_unrolled(blk, DMA_UNROLL, lambda r: start_scatter(i, r, slot))
            wait_scatters(slot)

    for parity in range(2):
        pl.when(jnp.logical_and(i < n_used, i % 2 == parity))(
            functools.partial(block_step, parity))


def _expert_call(blk_e, n_used, words, h, w_gu, b_gu, w_down, b_down, layer):
    t = h.shape[0]
    blk = EXPERT_BLOCK
    n_blocks = words.shape[0] // blk - 1
    depth, e, _, f2 = w_gu.shape
    f = w_down.shape[2]
    kern = functools.partial(_expert_kernel, trash_row0=TOP_K * t)
    return pl.pallas_call(
        kern,
        out_shape=jax.ShapeDtypeStruct((TOP_K * t + blk, D), F32),
        grid_spec=pltpu.PrefetchScalarGridSpec(
            num_scalar_prefetch=3,
            grid=(n_blocks,),
            in_specs=[
                pl.BlockSpec(memory_space=pl.ANY),
                pl.BlockSpec((None, None, D, f2), lambda i, be, nu, wo: (layer, be[i], 0, 0)),
                pl.BlockSpec((None, None, 1, f2), lambda i, be, nu, wo: (layer, be[i], 0, 0)),
                pl.BlockSpec((None, None, f, D), lambda i, be, nu, wo: (layer, be[i], 0, 0)),
                pl.BlockSpec((None, None, 1, D), lambda i, be, nu, wo: (layer, be[i], 0, 0)),
            ],
            out_specs=pl.BlockSpec(memory_space=pl.ANY),
            scratch_shapes=[
                pltpu.VMEM((D, f2), BF16), pltpu.VMEM((f, D), BF16),
                pltpu.VMEM((blk, D), F32), pltpu.VMEM((blk, D), F32),
                pltpu.VMEM((blk, D), F32), pltpu.VMEM((blk, D), F32),
                pltpu.SemaphoreType.DMA((2,)), pltpu.SemaphoreType.DMA((2,)),
            ],
        ),
        compiler_params=_cparams("arbitrary"),
        name="moe_experts",
    )(blk_e, n_used, words, h, w_gu, b_gu.reshape(depth, e, 1, f2), w_down,
      b_down.reshape(depth, e, 1, D))


def _combine_kernel(y0_ref, y1_ref, y2_ref, y3_ref, gates_ref, x_ref, g2_ref, lng_ref, lnb_ref,
                    o_ref):
    gates = gates_ref[...]
    y = gates[:, 0:1] * y0_ref[...]
    for k, y_ref in enumerate((y1_ref, y2_ref, y3_ref), start=1):
        y = y + gates[:, k:k + 1] * y_ref[...]
    z = DEEPNORM_ALPHA * x_ref[...] + g2_ref[...] * y
    o_ref[...] = _layer_norm_rows(z, lng_ref[...], lnb_ref[...])


def _combine_call(y, gates, xs, g2, ln_g, ln_b, mod_idx):
    t = xs.shape[0]
    tm = ROW_TILE
    nt = t // tm

    def y_spec(k):
        return pl.BlockSpec((tm, D), lambda i: (k * nt + i, 0))

    return pl.pallas_call(
        _combine_kernel,
        out_shape=jax.ShapeDtypeStruct((t, D), F32),
        grid=(nt,),
        in_specs=[y_spec(k) for k in range(TOP_K)] + [
            pl.BlockSpec((tm, LANES), lambda i: (i, 0)),
            pl.BlockSpec((tm, D), lambda i: (i, 0)),
            pl.BlockSpec((None, 1, D), mod_idx),
            pl.BlockSpec((1, D), lambda i: (0, 0)),
            pl.BlockSpec((1, D), lambda i: (0, 0)),
        ],
        out_specs=pl.BlockSpec((tm, D), lambda i: (i, 0)),
        compiler_params=_cparams("arbitrary"),
        name="moe_combine_ln",
    )(y, y, y, y, gates, xs, g2, ln_g.reshape(1, D), ln_b.reshape(1, D))


def _moe_plan(meta, counts, cap):
    blk = EXPERT_BLOCK
    t = meta.shape[0]
    assert TOP_K * t + blk <= 1 << SLOT_ROW_BITS and t <= 1 << (32 - SLOT_ROW_BITS)
    ids = meta[:, :TOP_K]
    rank = meta[:, TOP_K:2 * TOP_K]
    cnt = counts[0, :N_EXPERTS]
    padded = (cnt + blk - 1) // blk * blk
    pend = jnp.cumsum(padded)
    pstart = pend - padded
    dest = (pstart[ids] + rank).reshape(-1).astype(jnp.int32)
    tok = jnp.arange(t, dtype=jnp.int32)[:, None]
    real = (tok << SLOT_ROW_BITS) | (jnp.arange(TOP_K, dtype=jnp.int32)[None, :] * t + tok)
    pad_tok = np.uint32((t - 1) << SLOT_ROW_BITS).astype(np.int32)
    pad = pad_tok | (TOP_K * t + jnp.arange(blk + cap, dtype=jnp.int32) % blk)
    words = pad.astype(jnp.int32).at[blk + dest].set(real.reshape(-1), unique_indices=True)
    n_blocks = cap // blk
    starts = jnp.arange(n_blocks, dtype=pend.dtype) * blk
    blk_e = jnp.minimum(jnp.sum((pend[None, :] <= starts[:, None]).astype(jnp.int32), axis=1),
                        N_EXPERTS - 1).astype(jnp.int32)
    n_used = (pend[-1:] // blk).astype(jnp.int32)
    return words, blk_e, n_used


def kernel(x, c, ctx, c_ctx, w_mod, b_mod, ln_g, ln_b, attn_w_qkv, attn_lambda, attn_subln_g,
           attn_w_o, fn_w_in, fn_norm_g, fn_w_out, moe_w_router, moe_b_router, moe_w_gu,
           moe_b_gu, moe_w_down, moe_b_down):
    batch, n_lat, d = x.shape
    n_ctx = ctx.shape[1]
    assert d == D and batch == 2 and n_lat % DFT_TILE == 0 and n_ctx == ROW_TILE
    t_lat = batch * n_lat
    t_all = t_lat + batch * n_ctx
    lat_tiles = n_lat // ROW_TILE
    last_ctx_layer = max(i for i in range(DEPTH) if i % N_MIXERS == 0)

    def mod_idx(i):
        return (jnp.minimum(i // lat_tiles, batch), 0, 0)

    xs = jnp.concatenate([x.reshape(t_lat, D), ctx.reshape(batch * n_ctx, D)], axis=0)
    cc = jnp.concatenate([c, c_ctx[None, :], jnp.zeros((8 - batch - 1, D), F32)], axis=0)
    mods = _mod_call(cc, w_mod, b_mod)

    cos, sin = _rope_tables(n_lat, ROW_TILE)
    cs = jnp.concatenate(_dft_tables(FN_GROUP_DIM), axis=1).astype(BF16)
    ct_lat, st_lat = [m.astype(BF16) for m in _dft_tables(n_lat)]
    ct_ctx, st_ctx = [m.astype(BF16) for m in _dft_tables(n_ctx)]

    n_assign = t_all * TOP_K
    cap = -(-n_assign // EXPERT_BLOCK) * EXPERT_BLOCK + N_EXPERTS * EXPERT_BLOCK

    for i in range(DEPTH):
        j = i // N_MIXERS
        ctx_full = i < last_ctx_layer

        def mvec(k, i=i):
            return mods[i, :batch + 1, k * D:(k + 1) * D].reshape(batch + 1, 1, D)

        sh1, sc1, g1, sh2, sc2, g2 = [mvec(k) for k in range(6)]

        if i % N_MIXERS == 0:
            lambda_init = 0.8 - 0.6 * float(np.exp(-0.3 * i))
            q, k_all, vt_all = _qkv_call(xs, sc1, sh1, attn_w_qkv[j].astype(BF16), cos, sin,
                                         n_lat, n_ctx, batch)
            o_lat = _attn_call(q, k_all, vt_all, attn_lambda[j], attn_subln_g[j], lambda_init,
                               batch=batch, q_rows=n_lat, q_row0=0, tq=ATTN_Q_TILE,
                               chunk0=0, n_chunks=lat_tiles + 1)
            if ctx_full:
                o_ctx = _attn_call(q, k_all, vt_all, attn_lambda[j], attn_subln_g[j], lambda_init,
                                   batch=batch, q_rows=n_ctx, q_row0=t_lat, tq=n_ctx,
                                   chunk0=lat_tiles, n_chunks=1)
            else:
                o_ctx = jnp.zeros((batch * n_ctx, D), BF16)
            branch = jnp.concatenate([o_lat, o_ctx], axis=0)
            w_out = attn_w_o[j]
        else:
            a, bp = _fn_in_call(xs, sc1, sh1, fn_w_in[j].astype(BF16), fn_norm_g[j], cs, mod_idx)
            f_lat = _dft_call(ct_lat, st_lat, a, bp, seq=n_lat, row0=0, tile=DFT_TILE)
            if ctx_full:
                f_ctx = _dft_call(ct_ctx, st_ctx, a, bp, seq=n_ctx, row0=t_lat, tile=n_ctx)
            else:
                f_ctx = jnp.zeros((batch, n_ctx, D), BF16)
            branch = jnp.concatenate([f_lat.reshape(t_lat, D), f_ctx.reshape(batch * n_ctx, D)],
                                     axis=0)
            w_out = fn_w_out[j]
        xs = _proj_ln_call(branch, w_out.astype(BF16), xs, g1, ln_g[i, 0], ln_b[i, 0], mod_idx)

        h2, meta, gates, counts = _router_call(xs, sc2, sh2, moe_w_router[i], moe_b_router[i],
                                               mod_idx)
        words, blk_e, n_used = _moe_plan(meta, counts, cap)
        y = _expert_call(blk_e, n_used, words, h2, moe_w_gu, moe_b_gu, moe_w_down, moe_b_down, i)
        xs = _combine_call(y, gates, xs, g2, ln_g[i, 1], ln_b[i, 1], mod_idx)

    return xs[:t_lat].reshape(batch, n_lat, D)
```

```python
import functools

import jax
import jax.numpy as jnp
from jax import lax
import numpy as np
from jax.experimental import pallas as pl
from jax.experimental.pallas import tpu as pltpu

F32 = jnp.float32
BF16 = jnp.bfloat16
HIGHEST = lax.Precision.HIGHEST

D = 1024
GRID_W = 64
DEPTH = 4
N_MIXERS = 2
HEAD_DIM = 64
HEADS = 8
V_DIM = 128
ROPE_THETA = 10000.0
ROPE_FREQS = HEAD_DIM // 4
FN_GROUPS = 8
FN_GROUP_DIM = D // FN_GROUPS
N_EXPERTS = 32
TOP_K = 4
SWIGLU_LIMIT = 7.0
SWIGLU_ALPHA = 1.702
LN_EPS = 1e-5
RMS_EPS = 1e-5
DEEPNORM_ALPHA = (2 * DEPTH) ** 0.25

LANES = 128
TOKEN_TILE_ROWS = D // LANES
ROW_TILE = 256
ATTN_Q_TILE = 512
ATTN_K_TILE = 512
SUM_ROWS = 16
EXPERT_BLOCK = 512
Q_SCALE = HEAD_DIM ** -0.5 * float(np.log2(np.e))
FFT_RADIX = 16
DMA_UNROLL = 8
DMA_WAIT_UNROLL = 32
SLOT_ROW_BITS = 17
SLOT_ROW_MASK = (1 << SLOT_ROW_BITS) - 1
VMEM_LIMIT = 56 * 1024 * 1024


def _cparams(*sem):
    return pltpu.CompilerParams(dimension_semantics=sem, vmem_limit_bytes=VMEM_LIMIT)


def _loop_unrolled(n, unroll, fn):
    assert n % unroll == 0

    def body(g, carry):
        for u in range(unroll):
            fn(g * unroll + u)
        return carry
    lax.fori_loop(0, n // unroll, body, 0)


def _store_token_tiles(ref, value):
    rows = value.shape[0]
    for j in range(TOKEN_TILE_ROWS):
        ref[pl.ds(j, rows, stride=TOKEN_TILE_ROWS), :] = value[:, j * LANES:(j + 1) * LANES]


def _load_token_tiles(ref):
    rows = ref.shape[0] // TOKEN_TILE_ROWS
    return jnp.concatenate([ref[pl.ds(j, rows, stride=TOKEN_TILE_ROWS), :]
                            for j in range(TOKEN_TILE_ROWS)], axis=1)


def _layer_norm_rows(z, g, b):
    mu = jnp.mean(z, axis=-1, keepdims=True)
    zc = z - mu
    var = jnp.mean(zc * zc, axis=-1, keepdims=True)
    return zc * lax.rsqrt(var + LN_EPS) * g + b


def _mod_kernel(c_ref, w_ref, b_ref, o_ref):
    c = c_ref[...]
    s = c / (1.0 + jnp.exp(-c))
    o_ref[...] = jnp.dot(s, w_ref[...], precision=HIGHEST, preferred_element_type=F32) + b_ref[...]


def _mod_call(cc, w_mod, b_mod):
    depth, _, width = w_mod.shape
    tn = 1536
    return pl.pallas_call(
        _mod_kernel,
        out_shape=jax.ShapeDtypeStruct((depth, 8, width), F32),
        grid=(depth, width // tn),
        in_specs=[
            pl.BlockSpec((8, D), lambda l, j: (0, 0)),
            pl.BlockSpec((None, D, tn), lambda l, j: (l, 0, j)),
            pl.BlockSpec((None, 1, tn), lambda l, j: (l, 0, j)),
        ],
        out_specs=pl.BlockSpec((None, 8, tn), lambda l, j: (l, 0, j)),
        compiler_params=_cparams("arbitrary", "arbitrary"),
        name="mod",
    )(cc, w_mod, b_mod.reshape(depth, 1, width))


def _qkv_kernel(x_ref, sc_ref, sh_ref, w_ref, cos_ref, sin_ref, q_ref, k_ref, vt_ref):
    h = x_ref[...] * (1.0 + sc_ref[...]) + sh_ref[...]
    y = jnp.dot(h.astype(BF16), w_ref[...], preferred_element_type=F32)
    cos = cos_ref[...]
    sin = sin_ref[...]
    lane = lax.broadcasted_iota(jnp.int32, cos.shape, 1)
    first_half = (lane % (2 * ROPE_FREQS)) < ROPE_FREQS

    def rope(t):
        partner = jnp.where(first_half,
                            pltpu.roll(t, LANES - ROPE_FREQS, axis=1),
                            pltpu.roll(t, ROPE_FREQS, axis=1))
        return t * cos + partner * sin

    for j in range(D // LANES):
        cols = slice(j * LANES, (j + 1) * LANES)
        q_ref[:, cols] = (rope(y[:, cols]) * Q_SCALE).astype(BF16)
        k_ref[:, cols] = rope(y[:, D + j * LANES:D + (j + 1) * LANES]).astype(BF16)
    vt_ref[...] = y[:, 2 * D:].T.astype(BF16)


def _qkv_call(xs, sc, sh, w, cos, sin, n_lat, n_ctx, batch):
    t = xs.shape[0]
    tm = ROW_TILE
    lat_tiles = n_lat // tm
    all_lat = batch * lat_tiles

    def mod_idx(i):
        return (jnp.minimum(i // lat_tiles, batch), 0, 0)

    def rope_idx(i):
        return (jnp.where(i < all_lat, i % lat_tiles, lat_tiles), 0)

    def kv_idx(i):
        return (jnp.where(i < all_lat, i // lat_tiles, i - all_lat),
                jnp.where(i < all_lat, i % lat_tiles, lat_tiles), 0)

    def vt_idx(i):
        return kv_idx(i)[:2] + (0, 0)

    return pl.pallas_call(
        _qkv_kernel,
        out_shape=(jax.ShapeDtypeStruct((t, D), BF16),
                   jax.ShapeDtypeStruct((batch, n_lat + n_ctx, D), BF16),
                   jax.ShapeDtypeStruct((batch, lat_tiles + 1, D, tm), BF16)),
        grid=(t // tm,),
        in_specs=[
            pl.BlockSpec((tm, D), lambda i: (i, 0)),
            pl.BlockSpec((None, 1, D), mod_idx),
            pl.BlockSpec((None, 1, D), mod_idx),
            pl.BlockSpec((D, 3 * D), lambda i: (0, 0)),
            pl.BlockSpec((tm, LANES), rope_idx),
            pl.BlockSpec((tm, LANES), rope_idx),
        ],
        out_specs=(
            pl.BlockSpec((tm, D), lambda i: (i, 0)),
            pl.BlockSpec((None, tm, D), kv_idx),
            pl.BlockSpec((None, None, D, tm), vt_idx),
        ),
        compiler_params=_cparams("arbitrary"),
        name="qkv_rope",
    )(xs, sc, sh, w, cos, sin)


def _rope_tables(n_lat, pad_rows):
    rows = n_lat // GRID_W
    row_ids = jnp.repeat(jnp.arange(rows), GRID_W).astype(F32)
    col_ids = jnp.tile(jnp.arange(GRID_W), rows).astype(F32)
    inv_freq = ROPE_THETA ** (-jnp.arange(ROPE_FREQS, dtype=F32) / ROPE_FREQS)
    ar = row_ids[:, None] * inv_freq
    ac = col_ids[:, None] * inv_freq
    cos = jnp.concatenate([jnp.cos(ar), jnp.cos(ar), jnp.cos(ac), jnp.cos(ac)], axis=1)
    sin = jnp.concatenate([-jnp.sin(ar), jnp.sin(ar), -jnp.sin(ac), jnp.sin(ac)], axis=1)
    reps = LANES // HEAD_DIM
    cos = jnp.tile(cos, (1, reps))
    sin = jnp.tile(sin, (1, reps))
    cos = jnp.concatenate([cos, jnp.ones((pad_rows, LANES), F32)], axis=0)
    sin = jnp.concatenate([sin, jnp.zeros((pad_rows, LANES), F32)], axis=0)
    return cos, sin


def _attn_kernel(lam_ref, g_ref, q_ref, k_ref, vt_ref, o_ref, qq_ref, s_ref, m_ref, acc_ref, *,
                 n_chunks, chunks_per_step, lambda_init):
    tq = q_ref.shape[0]
    kc = vt_ref.shape[2]
    qt = q_ref[...].astype(F32).T
    sub = lax.broadcasted_iota(jnp.int32, qt.shape, 0)
    qq_ref[...] = jnp.concatenate([jnp.where(sub < HEAD_DIM, qt, 0.0),
                                   jnp.where(sub >= HEAD_DIM, qt, 0.0)], axis=1).astype(BF16)
    m_ref[...] = jnp.full(m_ref.shape, -jnp.inf, F32)
    acc_ref[...] = jnp.zeros(acc_ref.shape, F32)
    ones_rows = jnp.ones((SUM_ROWS, kc), BF16)

    def scores(slot, c0, nc):
        start = c0 * kc
        if not isinstance(start, int):
            start = pl.multiple_of(start, kc)
        kj = k_ref[pl.ds(start, nc * kc), :]
        s_ref[slot, 0:nc * kc, :] = jnp.dot(kj, qq_ref[...], preferred_element_type=F32)

    def accumulate(slot, c0, nc):
        s = s_ref[slot, 0:nc * kc, :]
        m_old = m_ref[...]
        m_new = jnp.maximum(m_old, jnp.max(s, axis=0, keepdims=True))
        alpha = jnp.exp2(m_old - m_new)
        pb = jnp.exp2(s - m_new).astype(BF16)
        pv = None
        for c in range(nc):
            v_ext = jnp.concatenate([vt_ref[c0 + c], ones_rows], axis=0)
            part = jnp.dot(v_ext, pb[c * kc:(c + 1) * kc], preferred_element_type=F32)
            pv = part if pv is None else pv + part
        acc_ref[...] = alpha * acc_ref[...] + pv
        m_ref[...] = m_new

    cps = chunks_per_step
    n_full = n_chunks // cps
    steps = [(i * cps, cps) for i in range(n_full)]
    if n_chunks % cps:
        steps.append((n_full * cps, n_chunks % cps))
    n_pairs = max(n_full - 1, 0) // 2
    scores(0, *steps[0])
    if n_pairs > 0:
        def body(jj, carry):
            c0 = jj * (2 * cps)
            scores(1, c0 + cps, cps)
            accumulate(0, c0, cps)
            scores(0, c0 + 2 * cps, cps)
            accumulate(1, c0 + cps, cps)
            return carry
        lax.fori_loop(0, n_pairs, body, 0)
    slot = 0
    for idx in range(2 * n_pairs, len(steps)):
        if idx + 1 < len(steps):
            scores(1 - slot, *steps[idx + 1])
        accumulate(slot, *steps[idx])
        slot = 1 - slot

    lam = lam_ref[...]
    lam_val = (jnp.exp(jnp.sum(lam[0:1] * lam[1:2], axis=1, keepdims=True))
               - jnp.exp(jnp.sum(lam[2:3] * lam[3:4], axis=1, keepdims=True)) + lambda_init)
    o_all = acc_ref[0:V_DIM, :] / acc_ref[V_DIM:V_DIM + 1, :]
    o = (o_all[:, :tq] - lam_val * o_all[:, tq:]).T
    o = o * lax.rsqrt(jnp.mean(o * o, axis=-1, keepdims=True) + RMS_EPS)
    o_ref[...] = (o * g_ref[...] * (1.0 - lambda_init)).astype(o_ref.dtype)


def _attn_call(q, k_all, vt_all, lam, subln_g, lambda_init, *, batch, q_rows, q_row0, tq,
               chunk0, n_chunks):
    nq = q_rows // tq
    q_blk0 = q_row0 // tq
    kc = vt_all.shape[3]
    key_blk = chunk0 // n_chunks
    kern = functools.partial(_attn_kernel, n_chunks=n_chunks,
                             chunks_per_step=ATTN_K_TILE // kc, lambda_init=lambda_init)
    return pl.pallas_call(
        kern,
        out_shape=jax.ShapeDtypeStruct((batch * q_rows, D), BF16),
        grid=(batch, HEADS, nq),
        in_specs=[
            pl.BlockSpec((4, HEAD_DIM), lambda b, h, i: (0, 0)),
            pl.BlockSpec((1, V_DIM), lambda b, h, i: (0, 0)),
            pl.BlockSpec((tq, V_DIM), lambda b, h, i: (q_blk0 + b * nq + i, h)),
            pl.BlockSpec((None, n_chunks * kc, V_DIM), lambda b, h, i: (b, key_blk, h)),
            pl.BlockSpec((None, n_chunks, V_DIM, kc), lambda b, h, i: (b, key_blk, h, 0)),
        ],
        out_specs=pl.BlockSpec((tq, V_DIM), lambda b, h, i: (b * nq + i, h)),
        scratch_shapes=[
            pltpu.VMEM((V_DIM, 2 * tq), BF16),
            pltpu.VMEM((2, ATTN_K_TILE, 2 * tq), F32),
            pltpu.VMEM((1, 2 * tq), F32),
            pltpu.VMEM((V_DIM + SUM_ROWS, 2 * tq), F32),
        ],
        compiler_params=_cparams("arbitrary", "arbitrary", "arbitrary"),
        name="diff_attn",
    )(lam, subln_g.reshape(1, V_DIM), q, k_all, vt_all)


def _proj_ln_kernel(a_ref, w_ref, x_ref, gate_ref, lng_ref, lnb_ref, o_ref):
    y = jnp.dot(a_ref[...], w_ref[...], preferred_element_type=F32)
    z = DEEPNORM_ALPHA * x_ref[...] + gate_ref[...] * y
    o_ref[...] = _layer_norm_rows(z, lng_ref[...], lnb_ref[...])


def _proj_ln_call(a, w, xs, gate, ln_g, ln_b, mod_idx):
    t = xs.shape[0]
    tm = ROW_TILE
    return pl.pallas_call(
        _proj_ln_kernel,
        out_shape=jax.ShapeDtypeStruct((t, D), F32),
        grid=(t // tm,),
        in_specs=[
            pl.BlockSpec((tm, D), lambda i: (i, 0)),
            pl.BlockSpec((D, D), lambda i: (0, 0)),
            pl.BlockSpec((tm, D), lambda i: (i, 0)),
            pl.BlockSpec((None, 1, D), mod_idx),
            pl.BlockSpec((1, D), lambda i: (0, 0)),
            pl.BlockSpec((1, D), lambda i: (0, 0)),
        ],
        out_specs=pl.BlockSpec((tm, D), lambda i: (i, 0)),
        compiler_params=_cparams("arbitrary"),
        name="proj_ln",
    )(a, w, xs, gate, ln_g.reshape(1, D), ln_b.reshape(1, D))


def _fn_in_kernel(x_ref, sc_ref, sh_ref, w_ref, g_ref, cs_ref, a_ref, b_ref, *scratch, radix):
    h = x_ref[...] * (1.0 + sc_ref[...]) + sh_ref[...]
    u = jnp.dot(h.astype(BF16), w_ref[...], preferred_element_type=F32)
    gain = g_ref[...]
    cs = cs_ref[...]
    for g in range(FN_GROUPS):
        cols = slice(g * FN_GROUP_DIM, (g + 1) * FN_GROUP_DIM)
        ug = u[:, cols]
        r = ug * lax.rsqrt(jnp.mean(ug * ug, axis=-1, keepdims=True) + RMS_EPS) * gain[:, cols]
        ab = jnp.dot(r.astype(BF16), cs, preferred_element_type=F32)
        if radix:
            scratch[0][g] = ab[:, :FN_GROUP_DIM]
            scratch[1][g] = ab[:, FN_GROUP_DIM:]
        else:
            a_ref[:, cols] = ab[:, :FN_GROUP_DIM].astype(BF16)
            b_ref[:, cols] = ab[:, FN_GROUP_DIM:].astype(BF16)
    if radix:
        rows = x_ref.shape[0] // radix
        for q in range(radix):
            for out_ref, scr in ((a_ref, scratch[0]), (b_ref, scratch[1])):
                out_ref[q] = jnp.concatenate(
                    [scr[g, pl.ds(q, rows, stride=radix), :] for g in range(FN_GROUPS)],
                    axis=1).astype(BF16)


def _fn_in_call(xs, sc, sh, w, norm_g, cs, *, row0, n_rows, seq, radix, mod_per_seq):
    tm = ROW_TILE
    blk0 = row0 // tm
    seq_tiles = seq // tm
    last_mod = sc.shape[0] - 1

    def mod_idx(i):
        return (i // seq_tiles if mod_per_seq else last_mod, 0, 0)

    if radix:
        out = jax.ShapeDtypeStruct((n_rows // seq, radix, seq // radix, D), BF16)
        out_spec = pl.BlockSpec((None, radix, tm // radix, D),
                                lambda i: (i // seq_tiles, 0, i % seq_tiles, 0))
        scratch = [pltpu.VMEM((FN_GROUPS, tm, FN_GROUP_DIM), F32),
                   pltpu.VMEM((FN_GROUPS, tm, FN_GROUP_DIM), F32)]
    else:
        out = jax.ShapeDtypeStruct((n_rows, D), BF16)
        out_spec = pl.BlockSpec((tm, D), lambda i: (i, 0))
        scratch = []
    return pl.pallas_call(
        functools.partial(_fn_in_kernel, radix=radix),
        out_shape=(out, out),
        grid=(n_rows // tm,),
        in_specs=[
            pl.BlockSpec((tm, D), lambda i: (blk0 + i, 0)),
            pl.BlockSpec((None, 1, D), mod_idx),
            pl.BlockSpec((None, 1, D), mod_idx),
            pl.BlockSpec((D, D), lambda i: (0, 0)),
            pl.BlockSpec((1, D), lambda i: (0, 0)),
            pl.BlockSpec((FN_GROUP_DIM, 2 * FN_GROUP_DIM), lambda i: (0, 0)),
        ],
        out_specs=(out_spec, out_spec),
        scratch_shapes=scratch,
        compiler_params=_cparams("arbitrary"),
        name="fn_in",
    )(xs, sc, sh, w, norm_g.reshape(1, D), cs)


def _fft_kernel(m_ref, cw_ref, sw_ref, a_ref, b_ref, o_ref, acc_ref, *, scale):
    radix, p, tc = a_ref.shape
    reps = tc // LANES
    cw1 = cw_ref[...]
    sw1 = sw_ref[...]
    cwa, swa = cw1, sw1
    for a in range(radix):
        v = jnp.concatenate([a_ref[a], b_ref[a]], axis=0)
        g = jnp.dot(m_ref[...], v, preferred_element_type=F32)
        gr, hi = g[:p], g[p:]
        if a == 0:
            tre, uim = gr, hi
        else:
            cwt = jnp.tile(cwa, (1, reps))
            swt = jnp.tile(swa, (1, reps))
            tre = gr * cwt - hi * swt
            uim = gr * swt + hi * cwt
            cwa, swa = cwa * cw1 - swa * sw1, swa * cw1 + cwa * sw1
        for c in range(radix):
            k8 = (8 * a * c) // radix if (8 * a * c) % radix == 0 else None
            ang = 2.0 * np.pi * a * c / radix
            co, si = float(np.cos(ang)), float(np.sin(ang))
            if k8 is not None and k8 % 2 == 0:
                co, si = [(1.0, 0.0), (0.0, 1.0), (-1.0, 0.0), (0.0, -1.0)][(k8 // 2) % 4]
            term = None
            if co != 0.0:
                term = tre if co == 1.0 else (-tre if co == -1.0 else co * tre)
            if si != 0.0:
                part = uim if si == 1.0 else (-uim if si == -1.0 else si * uim)
                term = -part if term is None else term - part
            if a == 0:
                acc_ref[c] = term
            else:
                acc_ref[c] += term
    o_ref[...] = (acc_ref[...] * scale).astype(o_ref.dtype)


def _fft_call(a, bp, radix):
    n_seq, _, p, _ = a.shape
    seq = radix * p
    tc = 2 * LANES
    cp, sp = _dft_tables(p)
    m = jnp.concatenate([jnp.concatenate([cp, -sp], axis=1),
                         jnp.concatenate([sp, cp], axis=1)], axis=0).astype(BF16)
    ang = jnp.arange(p, dtype=F32) * (2.0 * np.pi / seq)
    cw = jnp.broadcast_to(jnp.cos(ang)[:, None], (p, LANES))
    sw = jnp.broadcast_to(jnp.sin(ang)[:, None], (p, LANES))
    kern = functools.partial(_fft_kernel, scale=float((seq * FN_GROUP_DIM) ** -0.5))
    seq_spec = pl.BlockSpec((None, radix, p, tc), lambda b, j: (b, 0, 0, j))
    return pl.pallas_call(
        kern,
        out_shape=jax.ShapeDtypeStruct(a.shape, BF16),
        grid=(n_seq, D // tc),
        in_specs=[
            pl.BlockSpec((2 * p, 2 * p), lambda b, j: (0, 0)),
            pl.BlockSpec((p, LANES), lambda b, j: (0, 0)),
            pl.BlockSpec((p, LANES), lambda b, j: (0, 0)),
            seq_spec, seq_spec,
        ],
        out_specs=seq_spec,
        scratch_shapes=[pltpu.VMEM((radix, p, tc), F32)],
        compiler_params=_cparams("arbitrary", "arbitrary"),
        name="seq_fft",
    )(m, cw, sw, a, bp)


def _dft_tables(n):
    idx = jnp.arange(n, dtype=jnp.int32)
    prod = (idx[:, None] * idx[None, :]) % n
    ang = prod.astype(F32) * (2.0 * np.pi / n)
    return jnp.cos(ang), jnp.sin(ang)


def _dft_kernel(c_ref, s_ref, a0_ref, a1_ref, b0_ref, b1_ref, o_ref, acc_ref, *, scale):
    kj = pl.program_id(1)

    @pl.when(kj == 0)
    def _():
        acc_ref[...] = jnp.zeros(acc_ref.shape, F32)

    c = c_ref[...]
    s = s_ref[...]
    acc_ref[0] += (jnp.dot(c, a0_ref[...], preferred_element_type=F32)
                   - jnp.dot(s, b0_ref[...], preferred_element_type=F32))
    acc_ref[1] += (jnp.dot(c, a1_ref[...], preferred_element_type=F32)
                   - jnp.dot(s, b1_ref[...], preferred_element_type=F32))

    @pl.when(kj == pl.num_programs(1) - 1)
    def _():
        o_ref[...] = (acc_ref[...] * scale).astype(o_ref.dtype)


def _dft_call(ct, st, a, bp, *, seq, row0, tile):
    nt = seq // tile
    blk0 = row0 // tile
    kern = functools.partial(_dft_kernel, scale=float((seq * FN_GROUP_DIM) ** -0.5))
    seq_spec0 = pl.BlockSpec((tile, D), lambda i, j: (blk0 + j, 0))
    seq_spec1 = pl.BlockSpec((tile, D), lambda i, j: (blk0 + nt + j, 0))
    return pl.pallas_call(
        kern,
        out_shape=jax.ShapeDtypeStruct((2, seq, D), BF16),
        grid=(nt, nt),
        in_specs=[
            pl.BlockSpec((tile, tile), lambda i, j: (i, j)),
            pl.BlockSpec((tile, tile), lambda i, j: (i, j)),
            seq_spec0, seq_spec1, seq_spec0, seq_spec1,
        ],
        out_specs=pl.BlockSpec((2, tile, D), lambda i, j: (0, i, 0)),
        scratch_shapes=[pltpu.VMEM((2, tile, D), F32)],
        compiler_params=_cparams("arbitrary", "arbitrary"),
        name="seq_dft",
    )(ct, st, a, a, bp, bp)


def _router_kernel(x_ref, sc_ref, sh_ref, w_ref, b_ref, h_ref, meta_ref, gate_ref, cnt_ref,
                   carry_ref):
    i = pl.program_id(0)

    @pl.when(i == 0)
    def _():
        carry_ref[...] = jnp.zeros(carry_ref.shape, F32)

    h = x_ref[...] * (1.0 + sc_ref[...]) + sh_ref[...]
    _store_token_tiles(h_ref, h)
    logits = jnp.dot(h, w_ref[...], precision=HIGHEST, preferred_element_type=F32) + b_ref[...]
    tm = logits.shape[0]
    lane = lax.broadcasted_iota(jnp.int32, logits.shape, 1).astype(F32)
    wide = lax.broadcasted_iota(jnp.int32, (tm, LANES), 1)

    work = logits
    onehot = jnp.zeros(logits.shape, F32)
    picks, vals = [], []
    for _ in range(TOP_K):
        mx = jnp.max(work, axis=1, keepdims=True)
        idx = jnp.min(jnp.where(work == mx, lane, float(N_EXPERTS)), axis=1, keepdims=True)
        sel = lane == idx
        onehot = onehot + sel.astype(F32)
        work = jnp.where(sel, -jnp.inf, work)
        picks.append(idx)
        vals.append(mx)

    exps = [jnp.exp(v - vals[0]) for v in vals]
    denom = exps[0] + exps[1] + exps[2] + exps[3]

    r_i = lax.broadcasted_iota(jnp.int32, (tm, tm), 0)
    c_i = lax.broadcasted_iota(jnp.int32, (tm, tm), 1)
    tri = (c_i < r_i).astype(BF16)
    prefix = jnp.dot(tri, onehot.astype(BF16), preferred_element_type=F32) + carry_ref[0:1, 0:N_EXPERTS]

    meta = jnp.zeros((tm, LANES), jnp.int32)
    gates = jnp.zeros((tm, LANES), F32)
    for k in range(TOP_K):
        rank = jnp.sum(jnp.where(lane == picks[k], prefix, 0.0), axis=1, keepdims=True)
        meta = jnp.where(wide == k, picks[k].astype(jnp.int32), meta)
        meta = jnp.where(wide == TOP_K + k, rank.astype(jnp.int32), meta)
        gates = jnp.where(wide == k, exps[k] / denom, gates)
    meta_ref[...] = meta
    gate_ref[...] = gates

    total = carry_ref[0:1, 0:N_EXPERTS] + jnp.sum(onehot, axis=0, keepdims=True)
    carry_ref[0:1, 0:N_EXPERTS] = total
    cnt_ref[...] = jnp.zeros(cnt_ref.shape, jnp.int32)
    cnt_ref[0:1, 0:N_EXPERTS] = total.astype(jnp.int32)


def _router_call(xs, sc, sh, w, b, mod_idx):
    t = xs.shape[0]
    tm = ROW_TILE
    return pl.pallas_call(
        _router_kernel,
        out_shape=(jax.ShapeDtypeStruct((t * TOKEN_TILE_ROWS, LANES), F32),
                   jax.ShapeDtypeStruct((t, LANES), jnp.int32),
                   jax.ShapeDtypeStruct((t, LANES), F32),
                   jax.ShapeDtypeStruct((8, LANES), jnp.int32)),
        grid=(t // tm,),
        in_specs=[
            pl.BlockSpec((tm, D), lambda i: (i, 0)),
            pl.BlockSpec((None, 1, D), mod_idx),
            pl.BlockSpec((None, 1, D), mod_idx),
            pl.BlockSpec((D, N_EXPERTS), lambda i: (0, 0)),
            pl.BlockSpec((1, N_EXPERTS), lambda i: (0, 0)),
        ],
        out_specs=(pl.BlockSpec((tm * TOKEN_TILE_ROWS, LANES), lambda i: (i, 0)),
                   pl.BlockSpec((tm, LANES), lambda i: (i, 0)),
                   pl.BlockSpec((tm, LANES), lambda i: (i, 0)),
                   pl.BlockSpec((8, LANES), lambda i: (0, 0))),
        scratch_shapes=[pltpu.VMEM((8, LANES), F32)],
        compiler_params=_cparams("arbitrary"),
        name="router",
    )(xs, sc, sh, w, b.reshape(1, N_EXPERTS))


def _expert_kernel(blk_e_ref, n_used_ref, words_ref, h_ref, wgu_ref, bgu_ref, wd_ref, bd_ref,
                   y_ref, wgu_bf, wd_bf, xbuf0, xbuf1, ybuf0, ybuf1, gsem, ssem, *, trash_row0):
    i = pl.program_id(0)
    tr = TOKEN_TILE_ROWS
    blk = xbuf0.shape[0] // tr
    f = wd_ref.shape[0]
    n_used = n_used_ref[0]
    xbufs = (xbuf0, xbuf1)
    ybufs = (ybuf0, ybuf1)

    def word(j, r):
        return words_ref[(j + 1) * blk + r]

    def tile_rows(r):
        if not isinstance(r, int):
            return pl.ds(pl.multiple_of(r * tr, tr), tr)
        return pl.ds(r * tr, tr)

    def gather_copy(tok, r, dst):
        return pltpu.make_async_copy(h_ref.at[tile_rows(tok)], xbufs[dst].at[tile_rows(r)],
                                     gsem.at[dst])

    def scatter_copy(row, r, src):
        return pltpu.make_async_copy(ybufs[src].at[tile_rows(r)], y_ref.at[tile_rows(row)],
                                     ssem.at[src])

    def start_gather(j, r, dst):
        gather_copy(lax.shift_right_logical(word(j, r), SLOT_ROW_BITS), r, dst).start()

    def start_scatter(j, r, src):
        scatter_copy(word(j, r) & SLOT_ROW_MASK, r, src).start()

    def wait_gathers(dst):
        _loop_unrolled(blk, DMA_WAIT_UNROLL, lambda r: gather_copy(0, 0, dst).wait())

    def wait_scatters(src):
        _loop_unrolled(blk, DMA_WAIT_UNROLL, lambda r: scatter_copy(0, 0, src).wait())

    @pl.when(i == 0)
    def _():
        ybuf0[...] = jnp.zeros(ybuf0.shape, F32)
        ybuf1[...] = jnp.zeros(ybuf1.shape, F32)
        init = pltpu.make_async_copy(ybuf0, y_ref.at[pl.ds(trash_row0 * tr, blk * tr)],
                                     ssem.at[0])
        init.start()
        init.wait()
        _loop_unrolled(blk, DMA_UNROLL, lambda r: start_gather(0, r, 0))

    new_expert = jnp.logical_or(i == 0, blk_e_ref[i] != blk_e_ref[jnp.maximum(i - 1, 0)])

    @pl.when(jnp.logical_and(i < n_used, new_expert))
    def _():
        wgu_bf[...] = wgu_ref[...].astype(BF16)
        wd_bf[...] = wd_ref[...].astype(BF16)

    def block_step(slot):
        other = 1 - slot
        wait_gathers(slot)

        @pl.when(i >= 1)
        def _():
            wait_scatters(slot)

        nxt = jnp.minimum(i + 1, n_used - 1)

        def start_rows(part):
            for r in range(part * blk // 4, (part + 1) * blk // 4):
                start_gather(nxt, r, other)
                start_scatter(i - 1, r, other)

        start_rows(0)
        xb = _load_token_tiles(xbufs[slot]).astype(BF16)
        gu = jnp.dot(xb, wgu_bf[...], preferred_element_type=F32) + bgu_ref[...]
        start_rows(1)
        gate = jnp.minimum(gu[:, :f], SWIGLU_LIMIT)
        up = jnp.clip(gu[:, f:], -SWIGLU_LIMIT, SWIGLU_LIMIT)
        act = gate / (1.0 + jnp.exp(-SWIGLU_ALPHA * gate)) * (up + 1.0)
        start_rows(2)
        _store_token_tiles(ybufs[slot], jnp.dot(act.astype(BF16), wd_bf[...],
                                                preferred_element_type=F32) + bd_ref[...])
        start_rows(3)

        @pl.when(i == n_used - 1)
        def _():
            wait_gathers(other)
            wait_scatters(other)
            _loop_unrolled(blk, DMA_UNROLL, lambda r: start_scatter(i, r, slot))
            wait_scatters(slot)

    for parity in range(2):
        pl.when(jnp.logical_and(i < n_used, i % 2 == parity))(
            functools.partial(block_step, parity))


def _expert_call(blk_e, n_used, words, h, w_gu, b_gu, w_down, b_down, layer):
    t = h.shape[0] // TOKEN_TILE_ROWS
    blk = EXPERT_BLOCK
    n_blocks = words.shape[0] // blk - 1
    depth, e, _, f2 = w_gu.shape
    f = w_down.shape[2]
    kern = functools.partial(_expert_kernel, trash_row0=TOP_K * t)
    return pl.pallas_call(
        kern,
        out_shape=jax.ShapeDtypeStruct(((TOP_K * t + blk) * TOKEN_TILE_ROWS, LANES), F32),
        grid_spec=pltpu.PrefetchScalarGridSpec(
            num_scalar_prefetch=3,
            grid=(n_blocks,),
            in_specs=[
                pl.BlockSpec(memory_space=pl.ANY),
                pl.BlockSpec((None, None, D, f2), lambda i, be, nu, wo: (layer, be[i], 0, 0)),
                pl.BlockSpec((None, None, 1, f2), lambda i, be, nu, wo: (layer, be[i], 0, 0)),
                pl.BlockSpec((None, None, f, D), lambda i, be, nu, wo: (layer, be[i], 0, 0)),
                pl.BlockSpec((None, None, 1, D), lambda i, be, nu, wo: (layer, be[i], 0, 0)),
            ],
            out_specs=pl.BlockSpec(memory_space=pl.ANY),
            scratch_shapes=[
                pltpu.VMEM((D, f2), BF16), pltpu.VMEM((f, D), BF16),
                pltpu.VMEM((blk * TOKEN_TILE_ROWS, LANES), F32),
                pltpu.VMEM((blk * TOKEN_TILE_ROWS, LANES), F32),
                pltpu.VMEM((blk * TOKEN_TILE_ROWS, LANES), F32),
                pltpu.VMEM((blk * TOKEN_TILE_ROWS, LANES), F32),
                pltpu.SemaphoreType.DMA((2,)), pltpu.SemaphoreType.DMA((2,)),
            ],
        ),
        compiler_params=_cparams("arbitrary"),
        name="moe_experts",
    )(blk_e, n_used, words, h, w_gu, b_gu.reshape(depth, e, 1, f2), w_down,
      b_down.reshape(depth, e, 1, D))


def _combine_kernel(y0_ref, y1_ref, y2_ref, y3_ref, gates_ref, x_ref, g2_ref, lng_ref, lnb_ref,
                    o_ref):
    gates = gates_ref[...]
    y = gates[:, 0:1] * _load_token_tiles(y0_ref)
    for k, y_ref in enumerate((y1_ref, y2_ref, y3_ref), start=1):
        y = y + gates[:, k:k + 1] * _load_token_tiles(y_ref)
    z = DEEPNORM_ALPHA * x_ref[...] + g2_ref[...] * y
    o_ref[...] = _layer_norm_rows(z, lng_ref[...], lnb_ref[...])


def _combine_call(y, gates, xs, g2, ln_g, ln_b, mod_idx):
    t = xs.shape[0]
    tm = ROW_TILE
    nt = t // tm

    def y_spec(k):
        return pl.BlockSpec((tm * TOKEN_TILE_ROWS, LANES), lambda i: (k * nt + i, 0))

    return pl.pallas_call(
        _combine_kernel,
        out_shape=jax.ShapeDtypeStruct((t, D), F32),
        grid=(nt,),
        in_specs=[y_spec(k) for k in range(TOP_K)] + [
            pl.BlockSpec((tm, LANES), lambda i: (i, 0)),
            pl.BlockSpec((tm, D), lambda i: (i, 0)),
            pl.BlockSpec((None, 1, D), mod_idx),
            pl.BlockSpec((1, D), lambda i: (0, 0)),
            pl.BlockSpec((1, D), lambda i: (0, 0)),
        ],
        out_specs=pl.BlockSpec((tm, D), lambda i: (i, 0)),
        compiler_params=_cparams("arbitrary"),
        name="moe_combine_ln",
    )(y, y, y, y, gates, xs, g2, ln_g.reshape(1, D), ln_b.reshape(1, D))


def _moe_plan(meta, counts, cap):
    blk = EXPERT_BLOCK
    t = meta.shape[0]
    assert TOP_K * t + blk <= 1 << SLOT_ROW_BITS and t <= 1 << (32 - SLOT_ROW_BITS)
    ids = meta[:, :TOP_K]
    rank = meta[:, TOP_K:2 * TOP_K]
    cnt = counts[0, :N_EXPERTS]
    padded = (cnt + blk - 1) // blk * blk
    pend = jnp.cumsum(padded)
    pstart = pend - padded
    dest = (pstart[ids] + rank).reshape(-1).astype(jnp.int32)
    tok = jnp.arange(t, dtype=jnp.int32)[:, None]
    real = (tok << SLOT_ROW_BITS) | (jnp.arange(TOP_K, dtype=jnp.int32)[None, :] * t + tok)
    pad_tok = np.uint32((t - 1) << SLOT_ROW_BITS).astype(np.int32)
    pad = pad_tok | (TOP_K * t + jnp.arange(blk + cap, dtype=jnp.int32) % blk)
    words = pad.astype(jnp.int32).at[blk + dest].set(real.reshape(-1), unique_indices=True)
    n_blocks = cap // blk
    starts = jnp.arange(n_blocks, dtype=pend.dtype) * blk
    blk_e = jnp.minimum(jnp.sum((pend[None, :] <= starts[:, None]).astype(jnp.int32), axis=1),
                        N_EXPERTS - 1).astype(jnp.int32)
    n_used = (pend[-1:] // blk).astype(jnp.int32)
    return words, blk_e, n_used


def kernel(x, c, ctx, c_ctx, w_mod, b_mod, ln_g, ln_b, attn_w_qkv, attn_lambda, attn_subln_g,
           attn_w_o, fn_w_in, fn_norm_g, fn_w_out, moe_w_router, moe_b_router, moe_w_gu,
           moe_b_gu, moe_w_down, moe_b_down):
    batch, n_lat, d = x.shape
    n_ctx = ctx.shape[1]
    assert d == D and batch == 2 and n_lat % (4 * ROW_TILE) == 0 and n_ctx == ROW_TILE
    t_lat = batch * n_lat
    t_all = t_lat + batch * n_ctx
    lat_tiles = n_lat // ROW_TILE
    last_ctx_layer = max(i for i in range(DEPTH) if i % N_MIXERS == 0)

    def mod_idx(i):
        return (jnp.minimum(i // lat_tiles, batch), 0, 0)

    xs = jnp.concatenate([x.reshape(t_lat, D), ctx.reshape(batch * n_ctx, D)], axis=0)
    cc = jnp.concatenate([c, c_ctx[None, :], jnp.zeros((8 - batch - 1, D), F32)], axis=0)
    mods = _mod_call(cc, w_mod, b_mod)

    cos, sin = _rope_tables(n_lat, ROW_TILE)
    cs = jnp.concatenate(_dft_tables(FN_GROUP_DIM), axis=1).astype(BF16)
    ct_ctx, st_ctx = [m.astype(BF16) for m in _dft_tables(n_ctx)]

    n_assign = t_all * TOP_K
    cap = -(-n_assign // EXPERT_BLOCK) * EXPERT_BLOCK + N_EXPERTS * EXPERT_BLOCK

    for i in range(DEPTH):
        j = i // N_MIXERS
        ctx_full = i < last_ctx_layer

        def mvec(k, i=i):
            return mods[i, :batch + 1, k * D:(k + 1) * D].reshape(batch + 1, 1, D)

        sh1, sc1, g1, sh2, sc2, g2 = [mvec(k) for k in range(6)]

        if i % N_MIXERS == 0:
            lambda_init = 0.8 - 0.6 * float(np.exp(-0.3 * i))
            q, k_all, vt_all = _qkv_call(xs, sc1, sh1, attn_w_qkv[j].astype(BF16), cos, sin,
                                         n_lat, n_ctx, batch)
            o_lat = _attn_call(q, k_all, vt_all, attn_lambda[j], attn_subln_g[j], lambda_init,
                               batch=batch, q_rows=n_lat, q_row0=0, tq=ATTN_Q_TILE,
                               chunk0=0, n_chunks=lat_tiles + 1)
            if ctx_full:
                o_ctx = _attn_call(q, k_all, vt_all, attn_lambda[j], attn_subln_g[j], lambda_init,
                                   batch=batch, q_rows=n_ctx, q_row0=t_lat, tq=n_ctx,
                                   chunk0=lat_tiles, n_chunks=1)
            else:
                o_ctx = jnp.zeros((batch * n_ctx, D), BF16)
            branch = jnp.concatenate([o_lat, o_ctx], axis=0)
            w_out = attn_w_o[j]
        else:
            w_in = fn_w_in[j].astype(BF16)
            a, bp = _fn_in_call(xs, sc1, sh1, w_in, fn_norm_g[j], cs, row0=0, n_rows=t_lat,
                                seq=n_lat, radix=FFT_RADIX, mod_per_seq=True)
            f_lat = _fft_call(a, bp, FFT_RADIX)
            if ctx_full:
                a, bp = _fn_in_call(xs, sc1, sh1, w_in, fn_norm_g[j], cs, row0=t_lat,
                                    n_rows=batch * n_ctx, seq=n_ctx, radix=0, mod_per_seq=False)
                f_ctx = _dft_call(ct_ctx, st_ctx, a, bp, seq=n_ctx, row0=0, tile=n_ctx)
            else:
                f_ctx = jnp.zeros((batch, n_ctx, D), BF16)
            branch = jnp.concatenate([f_lat.reshape(t_lat, D), f_ctx.reshape(batch * n_ctx, D)],
                                     axis=0)
            w_out = fn_w_out[j]
        xs = _proj_ln_call(branch, w_out.astype(BF16), xs, g1, ln_g[i, 0], ln_b[i, 0], mod_idx)

        h2, meta, gates, counts = _router_call(xs, sc2, sh2, moe_w_router[i], moe_b_router[i],
                                               mod_idx)
        words, blk_e, n_used = _moe_plan(meta, counts, cap)
        y = _expert_call(blk_e, n_used, words, h2, moe_w_gu, moe_b_gu, moe_w_down, moe_b_down, i)
        xs = _combine_call(y, gates, xs, g2, ln_g[i, 1], ln_b[i, 1], mod_idx)

    return xs[:t_lat].reshape(batch, n_lat, D)
```

```python
import functools

import jax
import jax.numpy as jnp
from jax import lax
import numpy as np
from jax.experimental import pallas as pl
from jax.experimental.pallas import tpu as pltpu

F32 = jnp.float32
BF16 = jnp.bfloat16
HIGHEST = lax.Precision.HIGHEST

D = 1024
GRID_W = 64
DEPTH = 4
N_MIXERS = 2
HEAD_DIM = 64
HEADS = 8
V_DIM = 128
ROPE_THETA = 10000.0
ROPE_FREQS = HEAD_DIM // 4
FN_GROUPS = 8
FN_GROUP_DIM = D // FN_GROUPS
N_EXPERTS = 32
TOP_K = 4
SWIGLU_LIMIT = 7.0
SWIGLU_ALPHA = 1.702
LN_EPS = 1e-5
RMS_EPS = 1e-5
DEEPNORM_ALPHA = (2 * DEPTH) ** 0.25

LANES = 128
TOKEN_TILE_ROWS = D // LANES
ROW_TILE = 256
ATTN_Q_TILE = 1024
ATTN_K_TILE = 512
SUM_ROWS = 16
EXPERT_BLOCK = 512
Q_SCALE = HEAD_DIM ** -0.5 * float(np.log2(np.e))
FFT_RADIX = 16
DMA_UNROLL = 8
DMA_WAIT_UNROLL = 32
SLOT_ROW_BITS = 17
SLOT_ROW_MASK = (1 << SLOT_ROW_BITS) - 1
VMEM_LIMIT = 56 * 1024 * 1024


def _cparams(*sem):
    return pltpu.CompilerParams(dimension_semantics=sem, vmem_limit_bytes=VMEM_LIMIT)


def _loop_unrolled(n, unroll, fn):
    assert n % unroll == 0

    def body(g, carry):
        for u in range(unroll):
            fn(g * unroll + u)
        return carry
    lax.fori_loop(0, n // unroll, body, 0)


def _store_token_tiles(ref, value):
    rows = value.shape[0]
    for j in range(TOKEN_TILE_ROWS):
        ref[pl.ds(j, rows, stride=TOKEN_TILE_ROWS), :] = value[:, j * LANES:(j + 1) * LANES]


def _load_token_tiles(ref):
    rows = ref.shape[0] // TOKEN_TILE_ROWS
    return jnp.concatenate([ref[pl.ds(j, rows, stride=TOKEN_TILE_ROWS), :]
                            for j in range(TOKEN_TILE_ROWS)], axis=1)


def _layer_norm_rows(z, g, b):
    mu = jnp.mean(z, axis=-1, keepdims=True)
    zc = z - mu
    var = jnp.mean(zc * zc, axis=-1, keepdims=True)
    return zc * lax.rsqrt(var + LN_EPS) * g + b


def _mod_kernel(c_ref, w_ref, b_ref, o_ref):
    c = c_ref[...]
    s = c / (1.0 + jnp.exp(-c))
    o_ref[...] = jnp.dot(s, w_ref[...], precision=HIGHEST, preferred_element_type=F32) + b_ref[...]


def _mod_call(cc, w_mod, b_mod):
    depth, _, width = w_mod.shape
    tn = 1536
    return pl.pallas_call(
        _mod_kernel,
        out_shape=jax.ShapeDtypeStruct((depth, 8, width), F32),
        grid=(depth, width // tn),
        in_specs=[
            pl.BlockSpec((8, D), lambda l, j: (0, 0)),
            pl.BlockSpec((None, D, tn), lambda l, j: (l, 0, j)),
            pl.BlockSpec((None, 1, tn), lambda l, j: (l, 0, j)),
        ],
        out_specs=pl.BlockSpec((None, 8, tn), lambda l, j: (l, 0, j)),
        compiler_params=_cparams("arbitrary", "arbitrary"),
        name="mod",
    )(cc, w_mod, b_mod.reshape(depth, 1, width))


def _qkv_kernel(x_ref, sc_ref, sh_ref, w_ref, cos_ref, sin_ref, q_ref, k_ref, vt_ref):
    h = x_ref[...] * (1.0 + sc_ref[...]) + sh_ref[...]
    y = jnp.dot(h.astype(BF16), w_ref[...], preferred_element_type=F32)
    cos = cos_ref[...]
    sin = sin_ref[...]
    lane = lax.broadcasted_iota(jnp.int32, cos.shape, 1)
    first_half = (lane % (2 * ROPE_FREQS)) < ROPE_FREQS

    def rope(t):
        partner = jnp.where(first_half,
                            pltpu.roll(t, LANES - ROPE_FREQS, axis=1),
                            pltpu.roll(t, ROPE_FREQS, axis=1))
        return t * cos + partner * sin

    for j in range(D // LANES):
        cols = slice(j * LANES, (j + 1) * LANES)
        q_ref[:, cols] = (rope(y[:, cols]) * Q_SCALE).astype(BF16)
        k_ref[:, cols] = rope(y[:, D + j * LANES:D + (j + 1) * LANES]).astype(BF16)
    vt_ref[...] = y[:, 2 * D:].T.astype(BF16)


def _qkv_call(xs, sc, sh, w, cos, sin, n_lat, n_ctx, batch):
    t = xs.shape[0]
    tm = ROW_TILE
    lat_tiles = n_lat // tm
    all_lat = batch * lat_tiles

    def mod_idx(i):
        return (jnp.minimum(i // lat_tiles, batch), 0, 0)

    def rope_idx(i):
        return (jnp.where(i < all_lat, i % lat_tiles, lat_tiles), 0)

    def kv_idx(i):
        return (jnp.where(i < all_lat, i // lat_tiles, i - all_lat),
                jnp.where(i < all_lat, i % lat_tiles, lat_tiles), 0)

    def vt_idx(i):
        return kv_idx(i)[:2] + (0, 0)

    return pl.pallas_call(
        _qkv_kernel,
        out_shape=(jax.ShapeDtypeStruct((t, D), BF16),
                   jax.ShapeDtypeStruct((batch, n_lat + n_ctx, D), BF16),
                   jax.ShapeDtypeStruct((batch, lat_tiles + 1, D, tm), BF16)),
        grid=(t // tm,),
        in_specs=[
            pl.BlockSpec((tm, D), lambda i: (i, 0)),
            pl.BlockSpec((None, 1, D), mod_idx),
            pl.BlockSpec((None, 1, D), mod_idx),
            pl.BlockSpec((D, 3 * D), lambda i: (0, 0)),
            pl.BlockSpec((tm, LANES), rope_idx),
            pl.BlockSpec((tm, LANES), rope_idx),
        ],
        out_specs=(
            pl.BlockSpec((tm, D), lambda i: (i, 0)),
            pl.BlockSpec((None, tm, D), kv_idx),
            pl.BlockSpec((None, None, D, tm), vt_idx),
        ),
        compiler_params=_cparams("arbitrary"),
        name="qkv_rope",
    )(xs, sc, sh, w, cos, sin)


def _rope_tables(n_lat, pad_rows):
    rows = n_lat // GRID_W
    row_ids = jnp.repeat(jnp.arange(rows), GRID_W).astype(F32)
    col_ids = jnp.tile(jnp.arange(GRID_W), rows).astype(F32)
    inv_freq = ROPE_THETA ** (-jnp.arange(ROPE_FREQS, dtype=F32) / ROPE_FREQS)
    ar = row_ids[:, None] * inv_freq
    ac = col_ids[:, None] * inv_freq
    cos = jnp.concatenate([jnp.cos(ar), jnp.cos(ar), jnp.cos(ac), jnp.cos(ac)], axis=1)
    sin = jnp.concatenate([-jnp.sin(ar), jnp.sin(ar), -jnp.sin(ac), jnp.sin(ac)], axis=1)
    reps = LANES // HEAD_DIM
    cos = jnp.tile(cos, (1, reps))
    sin = jnp.tile(sin, (1, reps))
    cos = jnp.concatenate([cos, jnp.ones((pad_rows, LANES), F32)], axis=0)
    sin = jnp.concatenate([sin, jnp.zeros((pad_rows, LANES), F32)], axis=0)
    return cos, sin


def _attn_kernel(lam_ref, g_ref, q_ref, k_ref, vt_ref, o_ref, qq_ref, s_ref, m_ref, acc_ref, *,
                 n_chunks, chunks_per_step, lambda_init):
    tq = q_ref.shape[0]
    kc = vt_ref.shape[2]
    qt = q_ref[...].astype(F32).T
    sub = lax.broadcasted_iota(jnp.int32, qt.shape, 0)
    qq_ref[...] = jnp.concatenate([jnp.where(sub < HEAD_DIM, qt, 0.0),
                                   jnp.where(sub >= HEAD_DIM, qt, 0.0)], axis=1).astype(BF16)
    m_ref[...] = jnp.full(m_ref.shape, -jnp.inf, F32)
    acc_ref[...] = jnp.zeros(acc_ref.shape, F32)
    ones_rows = jnp.ones((SUM_ROWS, kc), BF16)

    def scores(slot, c0, nc):
        start = c0 * kc
        if not isinstance(start, int):
            start = pl.multiple_of(start, kc)
        kj = k_ref[pl.ds(start, nc * kc), :]
        s_ref[slot, 0:nc * kc, :] = jnp.dot(kj, qq_ref[...], preferred_element_type=F32)

    def accumulate(slot, c0, nc):
        s = s_ref[slot, 0:nc * kc, :]
        m_old = m_ref[...]
        m_new = jnp.maximum(m_old, jnp.max(s, axis=0, keepdims=True))
        alpha = jnp.exp2(m_old - m_new)
        pb = jnp.exp2(s - m_new).astype(BF16)
        pv = None
        for c in range(nc):
            v_ext = jnp.concatenate([vt_ref[c0 + c], ones_rows], axis=0)
            part = jnp.dot(v_ext, pb[c * kc:(c + 1) * kc], preferred_element_type=F32)
            pv = part if pv is None else pv + part
        acc_ref[...] = alpha * acc_ref[...] + pv
        m_ref[...] = m_new

    cps = chunks_per_step
    n_full = n_chunks // cps
    steps = [(i * cps, cps) for i in range(n_full)]
    if n_chunks % cps:
        steps.append((n_full * cps, n_chunks % cps))
    n_pairs = max(n_full - 1, 0) // 2
    scores(0, *steps[0])
    if n_pairs > 0:
        def body(jj, carry):
            c0 = jj * (2 * cps)
            scores(1, c0 + cps, cps)
            accumulate(0, c0, cps)
            scores(0, c0 + 2 * cps, cps)
            accumulate(1, c0 + cps, cps)
            return carry
        lax.fori_loop(0, n_pairs, body, 0)
    slot = 0
    for idx in range(2 * n_pairs, len(steps)):
        if idx + 1 < len(steps):
            scores(1 - slot, *steps[idx + 1])
        accumulate(slot, *steps[idx])
        slot = 1 - slot

    lam = lam_ref[...]
    lam_val = (jnp.exp(jnp.sum(lam[0:1] * lam[1:2], axis=1, keepdims=True))
               - jnp.exp(jnp.sum(lam[2:3] * lam[3:4], axis=1, keepdims=True)) + lambda_init)
    o_all = acc_ref[0:V_DIM, :] / acc_ref[V_DIM:V_DIM + 1, :]
    o = (o_all[:, :tq] - lam_val * o_all[:, tq:]).T
    o = o * lax.rsqrt(jnp.mean(o * o, axis=-1, keepdims=True) + RMS_EPS)
    o_ref[...] = (o * g_ref[...] * (1.0 - lambda_init)).astype(o_ref.dtype)


def _attn_call(q, k_all, vt_all, lam, subln_g, lambda_init, *, batch, q_rows, q_row0, tq,
               chunk0, n_chunks):
    nq = q_rows // tq
    q_blk0 = q_row0 // tq
    kc = vt_all.shape[3]
    key_blk = chunk0 // n_chunks
    kern = functools.partial(_attn_kernel, n_chunks=n_chunks,
                             chunks_per_step=ATTN_K_TILE // kc, lambda_init=lambda_init)
    return pl.pallas_call(
        kern,
        out_shape=jax.ShapeDtypeStruct((batch * q_rows, D), BF16),
        grid=(batch, HEADS, nq),
        in_specs=[
            pl.BlockSpec((4, HEAD_DIM), lambda b, h, i: (0, 0)),
            pl.BlockSpec((1, V_DIM), lambda b, h, i: (0, 0)),
            pl.BlockSpec((tq, V_DIM), lambda b, h, i: (q_blk0 + b * nq + i, h)),
            pl.BlockSpec((None, n_chunks * kc, V_DIM), lambda b, h, i: (b, key_blk, h)),
            pl.BlockSpec((None, n_chunks, V_DIM, kc), lambda b, h, i: (b, key_blk, h, 0)),
        ],
        out_specs=pl.BlockSpec((tq, V_DIM), lambda b, h, i: (b * nq + i, h)),
        scratch_shapes=[
            pltpu.VMEM((V_DIM, 2 * tq), BF16),
            pltpu.VMEM((2, ATTN_K_TILE, 2 * tq), F32),
            pltpu.VMEM((1, 2 * tq), F32),
            pltpu.VMEM((V_DIM + SUM_ROWS, 2 * tq), F32),
        ],
        compiler_params=_cparams("arbitrary", "arbitrary", "arbitrary"),
        name="diff_attn",
    )(lam, subln_g.reshape(1, V_DIM), q, k_all, vt_all)


def _proj_ln_kernel(a_ref, w_ref, x_ref, gate_ref, lng_ref, lnb_ref, o_ref):
    y = jnp.dot(a_ref[...], w_ref[...], preferred_element_type=F32)
    z = DEEPNORM_ALPHA * x_ref[...] + gate_ref[...] * y
    o_ref[...] = _layer_norm_rows(z, lng_ref[...], lnb_ref[...])


def _proj_ln_call(a, w, xs, gate, ln_g, ln_b, mod_idx):
    t = xs.shape[0]
    tm = ROW_TILE
    return pl.pallas_call(
        _proj_ln_kernel,
        out_shape=jax.ShapeDtypeStruct((t, D), F32),
        grid=(t // tm,),
        in_specs=[
            pl.BlockSpec((tm, D), lambda i: (i, 0)),
            pl.BlockSpec((D, D), lambda i: (0, 0)),
            pl.BlockSpec((tm, D), lambda i: (i, 0)),
            pl.BlockSpec((None, 1, D), mod_idx),
            pl.BlockSpec((1, D), lambda i: (0, 0)),
            pl.BlockSpec((1, D), lambda i: (0, 0)),
        ],
        out_specs=pl.BlockSpec((tm, D), lambda i: (i, 0)),
        compiler_params=_cparams("arbitrary"),
        name="proj_ln",
    )(a, w, xs, gate, ln_g.reshape(1, D), ln_b.reshape(1, D))


def _fn_in_kernel(x_ref, sc_ref, sh_ref, w_ref, g_ref, cs_ref, a_ref, b_ref, *scratch, radix):
    h = x_ref[...] * (1.0 + sc_ref[...]) + sh_ref[...]
    u = jnp.dot(h.astype(BF16), w_ref[...], preferred_element_type=F32)
    gain = g_ref[...]
    cs = cs_ref[...]
    for g in range(FN_GROUPS):
        cols = slice(g * FN_GROUP_DIM, (g + 1) * FN_GROUP_DIM)
        ug = u[:, cols]
        r = ug * lax.rsqrt(jnp.mean(ug * ug, axis=-1, keepdims=True) + RMS_EPS) * gain[:, cols]
        ab = jnp.dot(r.astype(BF16), cs, preferred_element_type=F32)
        if radix:
            scratch[0][g] = ab[:, :FN_GROUP_DIM]
            scratch[1][g] = ab[:, FN_GROUP_DIM:]
        else:
            a_ref[:, cols] = ab[:, :FN_GROUP_DIM].astype(BF16)
            b_ref[:, cols] = ab[:, FN_GROUP_DIM:].astype(BF16)
    if radix:
        rows = x_ref.shape[0] // radix
        for q in range(radix):
            for out_ref, scr in ((a_ref, scratch[0]), (b_ref, scratch[1])):
                out_ref[q] = jnp.concatenate(
                    [scr[g, pl.ds(q, rows, stride=radix), :] for g in range(FN_GROUPS)],
                    axis=1).astype(BF16)


def _fn_in_call(xs, sc, sh, w, norm_g, cs, *, row0, n_rows, seq, radix, mod_per_seq):
    tm = ROW_TILE
    blk0 = row0 // tm
    seq_tiles = seq // tm
    last_mod = sc.shape[0] - 1

    def mod_idx(i):
        return (i // seq_tiles if mod_per_seq else last_mod, 0, 0)

    if radix:
        out = jax.ShapeDtypeStruct((n_rows // seq, radix, seq // radix, D), BF16)
        out_spec = pl.BlockSpec((None, radix, tm // radix, D),
                                lambda i: (i // seq_tiles, 0, i % seq_tiles, 0))
        scratch = [pltpu.VMEM((FN_GROUPS, tm, FN_GROUP_DIM), F32),
                   pltpu.VMEM((FN_GROUPS, tm, FN_GROUP_DIM), F32)]
    else:
        out = jax.ShapeDtypeStruct((n_rows, D), BF16)
        out_spec = pl.BlockSpec((tm, D), lambda i: (i, 0))
        scratch = []
    return pl.pallas_call(
        functools.partial(_fn_in_kernel, radix=radix),
        out_shape=(out, out),
        grid=(n_rows // tm,),
        in_specs=[
            pl.BlockSpec((tm, D), lambda i: (blk0 + i, 0)),
            pl.BlockSpec((None, 1, D), mod_idx),
            pl.BlockSpec((None, 1, D), mod_idx),
            pl.BlockSpec((D, D), lambda i: (0, 0)),
            pl.BlockSpec((1, D), lambda i: (0, 0)),
            pl.BlockSpec((FN_GROUP_DIM, 2 * FN_GROUP_DIM), lambda i: (0, 0)),
        ],
        out_specs=(out_spec, out_spec),
        scratch_shapes=scratch,
        compiler_params=_cparams("arbitrary"),
        name="fn_in",
    )(xs, sc, sh, w, norm_g.reshape(1, D), cs)


def _fft_kernel(m_ref, cw_ref, sw_ref, a_ref, b_ref, o_ref, acc_ref, *, scale):
    radix, p, tc = a_ref.shape
    reps = tc // LANES
    cw1 = cw_ref[...]
    sw1 = sw_ref[...]
    cwa, swa = cw1, sw1
    for a in range(radix):
        v = jnp.concatenate([a_ref[a], b_ref[a]], axis=0)
        g = jnp.dot(m_ref[...], v, preferred_element_type=F32)
        gr, hi = g[:p], g[p:]
        if a == 0:
            tre, uim = gr, hi
        else:
            cwt = jnp.tile(cwa, (1, reps))
            swt = jnp.tile(swa, (1, reps))
            tre = gr * cwt - hi * swt
            uim = gr * swt + hi * cwt
            cwa, swa = cwa * cw1 - swa * sw1, swa * cw1 + cwa * sw1
        for c in range(radix):
            k8 = (8 * a * c) // radix if (8 * a * c) % radix == 0 else None
            ang = 2.0 * np.pi * a * c / radix
            co, si = float(np.cos(ang)), float(np.sin(ang))
            if k8 is not None and k8 % 2 == 0:
                co, si = [(1.0, 0.0), (0.0, 1.0), (-1.0, 0.0), (0.0, -1.0)][(k8 // 2) % 4]
            term = None
            if co != 0.0:
                term = tre if co == 1.0 else (-tre if co == -1.0 else co * tre)
            if si != 0.0:
                part = uim if si == 1.0 else (-uim if si == -1.0 else si * uim)
                term = -part if term is None else term - part
            if a == 0:
                acc_ref[c] = term
            else:
                acc_ref[c] += term
    o_ref[...] = (acc_ref[...] * scale).astype(o_ref.dtype)


def _fft_call(a, bp, radix):
    n_seq, _, p, _ = a.shape
    seq = radix * p
    tc = 2 * LANES
    cp, sp = _dft_tables(p)
    m = jnp.concatenate([jnp.concatenate([cp, -sp], axis=1),
                         jnp.concatenate([sp, cp], axis=1)], axis=0).astype(BF16)
    ang = jnp.arange(p, dtype=F32) * (2.0 * np.pi / seq)
    cw = jnp.broadcast_to(jnp.cos(ang)[:, None], (p, LANES))
    sw = jnp.broadcast_to(jnp.sin(ang)[:, None], (p, LANES))
    kern = functools.partial(_fft_kernel, scale=float((seq * FN_GROUP_DIM) ** -0.5))
    seq_spec = pl.BlockSpec((None, radix, p, tc), lambda b, j: (b, 0, 0, j))
    return pl.pallas_call(
        kern,
        out_shape=jax.ShapeDtypeStruct(a.shape, BF16),
        grid=(n_seq, D // tc),
        in_specs=[
            pl.BlockSpec((2 * p, 2 * p), lambda b, j: (0, 0)),
            pl.BlockSpec((p, LANES), lambda b, j: (0, 0)),
            pl.BlockSpec((p, LANES), lambda b, j: (0, 0)),
            seq_spec, seq_spec,
        ],
        out_specs=seq_spec,
        scratch_shapes=[pltpu.VMEM((radix, p, tc), F32)],
        compiler_params=_cparams("arbitrary", "arbitrary"),
        name="seq_fft",
    )(m, cw, sw, a, bp)


def _dft_tables(n):
    idx = jnp.arange(n, dtype=jnp.int32)
    prod = (idx[:, None] * idx[None, :]) % n
    ang = prod.astype(F32) * (2.0 * np.pi / n)
    return jnp.cos(ang), jnp.sin(ang)


def _dft_kernel(c_ref, s_ref, a0_ref, a1_ref, b0_ref, b1_ref, o_ref, acc_ref, *, scale):
    kj = pl.program_id(1)

    @pl.when(kj == 0)
    def _():
        acc_ref[...] = jnp.zeros(acc_ref.shape, F32)

    c = c_ref[...]
    s = s_ref[...]
    acc_ref[0] += (jnp.dot(c, a0_ref[...], preferred_element_type=F32)
                   - jnp.dot(s, b0_ref[...], preferred_element_type=F32))
    acc_ref[1] += (jnp.dot(c, a1_ref[...], preferred_element_type=F32)
                   - jnp.dot(s, b1_ref[...], preferred_element_type=F32))

    @pl.when(kj == pl.num_programs(1) - 1)
    def _():
        o_ref[...] = (acc_ref[...] * scale).astype(o_ref.dtype)


def _dft_call(ct, st, a, bp, *, seq, row0, tile):
    nt = seq // tile
    blk0 = row0 // tile
    kern = functools.partial(_dft_kernel, scale=float((seq * FN_GROUP_DIM) ** -0.5))
    seq_spec0 = pl.BlockSpec((tile, D), lambda i, j: (blk0 + j, 0))
    seq_spec1 = pl.BlockSpec((tile, D), lambda i, j: (blk0 + nt + j, 0))
    return pl.pallas_call(
        kern,
        out_shape=jax.ShapeDtypeStruct((2, seq, D), BF16),
        grid=(nt, nt),
        in_specs=[
            pl.BlockSpec((tile, tile), lambda i, j: (i, j)),
            pl.BlockSpec((tile, tile), lambda i, j: (i, j)),
            seq_spec0, seq_spec1, seq_spec0, seq_spec1,
        ],
        out_specs=pl.BlockSpec((2, tile, D), lambda i, j: (0, i, 0)),
        scratch_shapes=[pltpu.VMEM((2, tile, D), F32)],
        compiler_params=_cparams("arbitrary", "arbitrary"),
        name="seq_dft",
    )(ct, st, a, a, bp, bp)


def _router_kernel(x_ref, sc_ref, sh_ref, w_ref, b_ref, h_ref, meta_ref, gate_ref, cnt_ref,
                   carry_ref):
    i = pl.program_id(0)

    @pl.when(i == 0)
    def _():
        carry_ref[...] = jnp.zeros(carry_ref.shape, F32)

    h = x_ref[...] * (1.0 + sc_ref[...]) + sh_ref[...]
    _store_token_tiles(h_ref, h)
    logits = jnp.dot(h, w_ref[...], precision=HIGHEST, preferred_element_type=F32) + b_ref[...]
    tm = logits.shape[0]
    lane = lax.broadcasted_iota(jnp.int32, logits.shape, 1).astype(F32)
    wide = lax.broadcasted_iota(jnp.int32, (tm, LANES), 1)

    work = logits
    onehot = jnp.zeros(logits.shape, F32)
    picks, vals = [], []
    for _ in range(TOP_K):
        mx = jnp.max(work, axis=1, keepdims=True)
        idx = jnp.min(jnp.where(work == mx, lane, float(N_EXPERTS)), axis=1, keepdims=True)
        sel = lane == idx
        onehot = onehot + sel.astype(F32)
        work = jnp.where(sel, -jnp.inf, work)
        picks.append(idx)
        vals.append(mx)

    exps = [jnp.exp(v - vals[0]) for v in vals]
    denom = exps[0] + exps[1] + exps[2] + exps[3]

    r_i = lax.broadcasted_iota(jnp.int32, (tm, tm), 0)
    c_i = lax.broadcasted_iota(jnp.int32, (tm, tm), 1)
    tri = (c_i < r_i).astype(BF16)
    prefix = jnp.dot(tri, onehot.astype(BF16), preferred_element_type=F32) + carry_ref[0:1, 0:N_EXPERTS]

    meta = jnp.zeros((tm, LANES), jnp.int32)
    gates = jnp.zeros((tm, LANES), F32)
    for k in range(TOP_K):
        rank = jnp.sum(jnp.where(lane == picks[k], prefix, 0.0), axis=1, keepdims=True)
        meta = jnp.where(wide == k, picks[k].astype(jnp.int32), meta)
        meta = jnp.where(wide == TOP_K + k, rank.astype(jnp.int32), meta)
        gates = jnp.where(wide == k, exps[k] / denom, gates)
    meta_ref[...] = meta
    gate_ref[...] = gates

    total = carry_ref[0:1, 0:N_EXPERTS] + jnp.sum(onehot, axis=0, keepdims=True)
    carry_ref[0:1, 0:N_EXPERTS] = total
    cnt_ref[...] = jnp.zeros(cnt_ref.shape, jnp.int32)
    cnt_ref[0:1, 0:N_EXPERTS] = total.astype(jnp.int32)


def _router_call(xs, sc, sh, w, b, mod_idx):
    t = xs.shape[0]
    tm = ROW_TILE
    return pl.pallas_call(
        _router_kernel,
        out_shape=(jax.ShapeDtypeStruct((t * TOKEN_TILE_ROWS, LANES), F32),
                   jax.ShapeDtypeStruct((t, LANES), jnp.int32),
                   jax.ShapeDtypeStruct((t, LANES), F32),
                   jax.ShapeDtypeStruct((8, LANES), jnp.int32)),
        grid=(t // tm,),
        in_specs=[
            pl.BlockSpec((tm, D), lambda i: (i, 0)),
            pl.BlockSpec((None, 1, D), mod_idx),
            pl.BlockSpec((None, 1, D), mod_idx),
            pl.BlockSpec((D, N_EXPERTS), lambda i: (0, 0)),
            pl.BlockSpec((1, N_EXPERTS), lambda i: (0, 0)),
        ],
        out_specs=(pl.BlockSpec((tm * TOKEN_TILE_ROWS, LANES), lambda i: (i, 0)),
                   pl.BlockSpec((tm, LANES), lambda i: (i, 0)),
                   pl.BlockSpec((tm, LANES), lambda i: (i, 0)),
                   pl.BlockSpec((8, LANES), lambda i: (0, 0))),
        scratch_shapes=[pltpu.VMEM((8, LANES), F32)],
        compiler_params=_cparams("arbitrary"),
        name="router",
    )(xs, sc, sh, w, b.reshape(1, N_EXPERTS))


def _expert_kernel(blk_e_ref, n_used_ref, words_ref, h_ref, wgu_ref, bgu_ref, wd_ref, bd_ref,
                   y_ref, wgu_bf, wd_bf, xbuf0, xbuf1, ybuf0, ybuf1, gsem, ssem, *, trash_row0):
    i = pl.program_id(0)
    tr = TOKEN_TILE_ROWS
    blk = xbuf0.shape[0] // tr
    f = wd_ref.shape[0]
    n_used = n_used_ref[0]
    xbufs = (xbuf0, xbuf1)
    ybufs = (ybuf0, ybuf1)

    def word(j, r):
        return words_ref[(j + 1) * blk + r]

    def tile_rows(r):
        if not isinstance(r, int):
            return pl.ds(pl.multiple_of(r * tr, tr), tr)
        return pl.ds(r * tr, tr)

    def gather_copy(tok, r, dst):
        return pltpu.make_async_copy(h_ref.at[tile_rows(tok)], xbufs[dst].at[tile_rows(r)],
                                     gsem.at[dst])

    def scatter_copy(row, r, src):
        return pltpu.make_async_copy(ybufs[src].at[tile_rows(r)], y_ref.at[tile_rows(row)],
                                     ssem.at[src])

    def start_gather(j, r, dst):
        prio = r % 2 if isinstance(r, int) else 0
        gather_copy(lax.shift_right_logical(word(j, r), SLOT_ROW_BITS), r, dst).start(priority=prio)

    def start_scatter(j, r, src):
        prio = r % 2 if isinstance(r, int) else 0
        scatter_copy(word(j, r) & SLOT_ROW_MASK, r, src).start(priority=prio)

    def wait_gathers(dst):
        _loop_unrolled(blk, DMA_WAIT_UNROLL, lambda r: gather_copy(0, 0, dst).wait())

    def wait_scatters(src):
        _loop_unrolled(blk, DMA_WAIT_UNROLL, lambda r: scatter_copy(0, 0, src).wait())

    @pl.when(i == 0)
    def _():
        ybuf0[...] = jnp.zeros(ybuf0.shape, F32)
        ybuf1[...] = jnp.zeros(ybuf1.shape, F32)
        init = pltpu.make_async_copy(ybuf0, y_ref.at[pl.ds(trash_row0 * tr, blk * tr)],
                                     ssem.at[0])
        init.start()
        init.wait()
        _loop_unrolled(blk, DMA_UNROLL, lambda r: start_gather(0, r, 0))

    new_expert = jnp.logical_or(i == 0, blk_e_ref[i] != blk_e_ref[jnp.maximum(i - 1, 0)])

    @pl.when(jnp.logical_and(i < n_used, new_expert))
    def _():
        wgu_bf[...] = wgu_ref[...].astype(BF16)
        wd_bf[...] = wd_ref[...].astype(BF16)

    def block_step(slot):
        other = 1 - slot
        wait_gathers(slot)

        @pl.when(i >= 1)
        def _():
            wait_scatters(slot)

        nxt = jnp.minimum(i + 1, n_used - 1)

        def start_rows(part):
            for r in range(part * blk // 4, (part + 1) * blk // 4):
                start_gather(nxt, r, other)
                start_scatter(i - 1, r, other)

        start_rows(0)
        xb = _load_token_tiles(xbufs[slot]).astype(BF16)
        gu = jnp.dot(xb, wgu_bf[...], preferred_element_type=F32) + bgu_ref[...]
        start_rows(1)
        gate = jnp.minimum(gu[:, :f], SWIGLU_LIMIT)
        up = jnp.clip(gu[:, f:], -SWIGLU_LIMIT, SWIGLU_LIMIT)
        act = gate / (1.0 + jnp.exp(-SWIGLU_ALPHA * gate)) * (up + 1.0)
        start_rows(2)
        _store_token_tiles(ybufs[slot], jnp.dot(act.astype(BF16), wd_bf[...],
                                                preferred_element_type=F32) + bd_ref[...])
        start_rows(3)

        @pl.when(i == n_used - 1)
        def _():
            wait_gathers(other)
            wait_scatters(other)
            _loop_unrolled(blk, DMA_UNROLL, lambda r: start_scatter(i, r, slot))
            wait_scatters(slot)

    for parity in range(2):
        pl.when(jnp.logical_and(i < n_used, i % 2 == parity))(
            functools.partial(block_step, parity))


def _expert_call(blk_e, n_used, words, h, w_gu, b_gu, w_down, b_down, layer):
    t = h.shape[0] // TOKEN_TILE_ROWS
    blk = EXPERT_BLOCK
    n_blocks = words.shape[0] // blk - 1
    depth, e, _, f2 = w_gu.shape
    f = w_down.shape[2]
    kern = functools.partial(_expert_kernel, trash_row0=TOP_K * t)
    return pl.pallas_call(
        kern,
        out_shape=jax.ShapeDtypeStruct(((TOP_K * t + blk) * TOKEN_TILE_ROWS, LANES), F32),
        grid_spec=pltpu.PrefetchScalarGridSpec(
            num_scalar_prefetch=3,
            grid=(n_blocks,),
            in_specs=[
                pl.BlockSpec(memory_space=pl.ANY),
                pl.BlockSpec((None, None, D, f2), lambda i, be, nu, wo: (layer, be[i], 0, 0)),
                pl.BlockSpec((None, None, 1, f2), lambda i, be, nu, wo: (layer, be[i], 0, 0)),
                pl.BlockSpec((None, None, f, D), lambda i, be, nu, wo: (layer, be[i], 0, 0)),
                pl.BlockSpec((None, None, 1, D), lambda i, be, nu, wo: (layer, be[i], 0, 0)),
            ],
            out_specs=pl.BlockSpec(memory_space=pl.ANY),
            scratch_shapes=[
                pltpu.VMEM((D, f2), BF16), pltpu.VMEM((f, D), BF16),
                pltpu.VMEM((blk * TOKEN_TILE_ROWS, LANES), F32),
                pltpu.VMEM((blk * TOKEN_TILE_ROWS, LANES), F32),
                pltpu.VMEM((blk * TOKEN_TILE_ROWS, LANES), F32),
                pltpu.VMEM((blk * TOKEN_TILE_ROWS, LANES), F32),
                pltpu.SemaphoreType.DMA((2,)), pltpu.SemaphoreType.DMA((2,)),
            ],
        ),
        compiler_params=_cparams("arbitrary"),
        name="moe_experts",
    )(blk_e, n_used, words, h, w_gu, b_gu.reshape(depth, e, 1, f2), w_down,
      b_down.reshape(depth, e, 1, D))


def _combine_kernel(y0_ref, y1_ref, y2_ref, y3_ref, gates_ref, x_ref, g2_ref, lng_ref, lnb_ref,
                    o_ref):
    gates = gates_ref[...]
    y = gates[:, 0:1] * _load_token_tiles(y0_ref)
    for k, y_ref in enumerate((y1_ref, y2_ref, y3_ref), start=1):
        y = y + gates[:, k:k + 1] * _load_token_tiles(y_ref)
    z = DEEPNORM_ALPHA * x_ref[...] + g2_ref[...] * y
    o_ref[...] = _layer_norm_rows(z, lng_ref[...], lnb_ref[...])


def _combine_call(y, gates, xs, g2, ln_g, ln_b, mod_idx):
    t = xs.shape[0]
    tm = ROW_TILE
    nt = t // tm

    def y_spec(k):
        return pl.BlockSpec((tm * TOKEN_TILE_ROWS, LANES), lambda i: (k * nt + i, 0))

    return pl.pallas_call(
        _combine_kernel,
        out_shape=jax.ShapeDtypeStruct((t, D), F32),
        grid=(nt,),
        in_specs=[y_spec(k) for k in range(TOP_K)] + [
            pl.BlockSpec((tm, LANES), lambda i: (i, 0)),
            pl.BlockSpec((tm, D), lambda i: (i, 0)),
            pl.BlockSpec((None, 1, D), mod_idx),
            pl.BlockSpec((1, D), lambda i: (0, 0)),
            pl.BlockSpec((1, D), lambda i: (0, 0)),
        ],
        out_specs=pl.BlockSpec((tm, D), lambda i: (i, 0)),
        compiler_params=_cparams("arbitrary"),
        name="moe_combine_ln",
    )(y, y, y, y, gates, xs, g2, ln_g.reshape(1, D), ln_b.reshape(1, D))


def _moe_plan(meta, counts, cap):
    blk = EXPERT_BLOCK
    t = meta.shape[0]
    assert TOP_K * t + blk <= 1 << SLOT_ROW_BITS and t <= 1 << (32 - SLOT_ROW_BITS)
    ids = meta[:, :TOP_K]
    rank = meta[:, TOP_K:2 * TOP_K]
    cnt = counts[0, :N_EXPERTS]
    padded = (cnt + blk - 1) // blk * blk
    pend = jnp.cumsum(padded)
    pstart = pend - padded
    dest = (pstart[ids] + rank).reshape(-1).astype(jnp.int32)
    tok = jnp.arange(t, dtype=jnp.int32)[:, None]
    real = (tok << SLOT_ROW_BITS) | (jnp.arange(TOP_K, dtype=jnp.int32)[None, :] * t + tok)
    pad_tok = np.uint32((t - 1) << SLOT_ROW_BITS).astype(np.int32)
    pad = pad_tok | (TOP_K * t + jnp.arange(blk + cap, dtype=jnp.int32) % blk)
    words = pad.astype(jnp.int32).at[blk + dest].set(real.reshape(-1), unique_indices=True)
    n_blocks = cap // blk
    starts = jnp.arange(n_blocks, dtype=pend.dtype) * blk
    blk_e = jnp.minimum(jnp.sum((pend[None, :] <= starts[:, None]).astype(jnp.int32), axis=1),
                        N_EXPERTS - 1).astype(jnp.int32)
    n_used = (pend[-1:] // blk).astype(jnp.int32)
    return words, blk_e, n_used


def kernel(x, c, ctx, c_ctx, w_mod, b_mod, ln_g, ln_b, attn_w_qkv, attn_lambda, attn_subln_g,
           attn_w_o, fn_w_in, fn_norm_g, fn_w_out, moe_w_router, moe_b_router, moe_w_gu,
           moe_b_gu, moe_w_down, moe_b_down):
    batch, n_lat, d = x.shape
    n_ctx = ctx.shape[1]
    assert d == D and batch == 2 and n_lat % (4 * ROW_TILE) == 0 and n_ctx == ROW_TILE
    t_lat = batch * n_lat
    t_all = t_lat + batch * n_ctx
    lat_tiles = n_lat // ROW_TILE
    last_ctx_layer = max(i for i in range(DEPTH) if i % N_MIXERS == 0)

    def mod_idx(i):
        return (jnp.minimum(i // lat_tiles, batch), 0, 0)

    xs = jnp.concatenate([x.reshape(t_lat, D), ctx.reshape(batch * n_ctx, D)], axis=0)
    cc = jnp.concatenate([c, c_ctx[None, :], jnp.zeros((8 - batch - 1, D), F32)], axis=0)
    mods = _mod_call(cc, w_mod, b_mod)

    cos, sin = _rope_tables(n_lat, ROW_TILE)
    cs = jnp.concatenate(_dft_tables(FN_GROUP_DIM), axis=1).astype(BF16)
    ct_ctx, st_ctx = [m.astype(BF16) for m in _dft_tables(n_ctx)]

    n_assign = t_all * TOP_K
    cap = -(-n_assign // EXPERT_BLOCK) * EXPERT_BLOCK + N_EXPERTS * EXPERT_BLOCK

    for i in range(DEPTH):
        j = i // N_MIXERS
        ctx_full = i < last_ctx_layer

        def mvec(k, i=i):
            return mods[i, :batch + 1, k * D:(k + 1) * D].reshape(batch + 1, 1, D)

        sh1, sc1, g1, sh2, sc2, g2 = [mvec(k) for k in range(6)]

        if i % N_MIXERS == 0:
            lambda_init = 0.8 - 0.6 * float(np.exp(-0.3 * i))
            q, k_all, vt_all = _qkv_call(xs, sc1, sh1, attn_w_qkv[j].astype(BF16), cos, sin,
                                         n_lat, n_ctx, batch)
            o_lat = _attn_call(q, k_all, vt_all, attn_lambda[j], attn_subln_g[j], lambda_init,
                               batch=batch, q_rows=n_lat, q_row0=0, tq=ATTN_Q_TILE,
                               chunk0=0, n_chunks=lat_tiles + 1)
            if ctx_full:
                o_ctx = _attn_call(q, k_all, vt_all, attn_lambda[j], attn_subln_g[j], lambda_init,
                                   batch=batch, q_rows=n_ctx, q_row0=t_lat, tq=n_ctx,
                                   chunk0=lat_tiles, n_chunks=1)
            else:
                o_ctx = jnp.zeros((batch * n_ctx, D), BF16)
            branch = jnp.concatenate([o_lat, o_ctx], axis=0)
            w_out = attn_w_o[j]
        else:
            w_in = fn_w_in[j].astype(BF16)
            a, bp = _fn_in_call(xs, sc1, sh1, w_in, fn_norm_g[j], cs, row0=0, n_rows=t_lat,
                                seq=n_lat, radix=FFT_RADIX, mod_per_seq=True)
            f_lat = _fft_call(a, bp, FFT_RADIX)
            if ctx_full:
                a, bp = _fn_in_call(xs, sc1, sh1, w_in, fn_norm_g[j], cs, row0=t_lat,
                                    n_rows=batch * n_ctx, seq=n_ctx, radix=0, mod_per_seq=False)
                f_ctx = _dft_call(ct_ctx, st_ctx, a, bp, seq=n_ctx, row0=0, tile=n_ctx)
            else:
                f_ctx = jnp.zeros((batch, n_ctx, D), BF16)
            branch = jnp.concatenate([f_lat.reshape(t_lat, D), f_ctx.reshape(batch * n_ctx, D)],
                                     axis=0)
            w_out = fn_w_out[j]
        xs = _proj_ln_call(branch, w_out.astype(BF16), xs, g1, ln_g[i, 0], ln_b[i, 0], mod_idx)

        h2, meta, gates, counts = _router_call(xs, sc2, sh2, moe_w_router[i], moe_b_router[i],
                                               mod_idx)
        words, blk_e, n_used = _moe_plan(meta, counts, cap)
        y = _expert_call(blk_e, n_used, words, h2, moe_w_gu, moe_b_gu, moe_w_down, moe_b_down, i)
        xs = _combine_call(y, gates, xs, g2, ln_g[i, 1], ln_b[i, 1], mod_idx)

    return xs[:t_lat].reshape(batch, n_lat, D)
```

```python
import functools

import jax
import jax.numpy as jnp
from jax import lax
import numpy as np
from jax.experimental import pallas as pl
from jax.experimental.pallas import tpu as pltpu

F32 = jnp.float32
BF16 = jnp.bfloat16
HIGHEST = lax.Precision.HIGHEST

D = 1024
GRID_W = 64
DEPTH = 4
N_MIXERS = 2
HEAD_DIM = 64
HEADS = 8
V_DIM = 128
ROPE_THETA = 10000.0
ROPE_FREQS = HEAD_DIM // 4
FN_GROUPS = 8
FN_GROUP_DIM = D // FN_GROUPS
N_EXPERTS = 32
TOP_K = 4
SWIGLU_LIMIT = 7.0
SWIGLU_ALPHA = 1.702
LN_EPS = 1e-5
RMS_EPS = 1e-5
DEEPNORM_ALPHA = (2 * DEPTH) ** 0.25

LANES = 128
TOKEN_TILE_ROWS = D // LANES
ROW_TILE = 256
ATTN_Q_TILE = 1024
ATTN_K_TILE = 1024
SUM_ROWS = 16
EXPERT_BLOCK = 512
Q_SCALE = HEAD_DIM ** -0.5 * float(np.log2(np.e))
FFT_RADIX = 16
DMA_UNROLL = 8
DMA_WAIT_UNROLL = 32
SLOT_ROW_BITS = 17
SLOT_ROW_MASK = (1 << SLOT_ROW_BITS) - 1
VMEM_LIMIT = 56 * 1024 * 1024


def _cparams(*sem):
    return pltpu.CompilerParams(dimension_semantics=sem, vmem_limit_bytes=VMEM_LIMIT)


def _loop_unrolled(n, unroll, fn):
    assert n % unroll == 0

    def body(g, carry):
        for u in range(unroll):
            fn(g * unroll + u)
        return carry
    lax.fori_loop(0, n // unroll, body, 0)


def _store_token_tiles(ref, value):
    rows = value.shape[0]
    for j in range(TOKEN_TILE_ROWS):
        ref[pl.ds(j, rows, stride=TOKEN_TILE_ROWS), :] = value[:, j * LANES:(j + 1) * LANES]


def _load_token_tiles(ref):
    rows = ref.shape[0] // TOKEN_TILE_ROWS
    return jnp.concatenate([ref[pl.ds(j, rows, stride=TOKEN_TILE_ROWS), :]
                            for j in range(TOKEN_TILE_ROWS)], axis=1)


def _layer_norm_rows(z, g, b):
    mu = jnp.mean(z, axis=-1, keepdims=True)
    zc = z - mu
    var = jnp.mean(zc * zc, axis=-1, keepdims=True)
    return zc * lax.rsqrt(var + LN_EPS) * g + b


def _mod_kernel(c_ref, w_ref, b_ref, o_ref):
    c = c_ref[...]
    s = c / (1.0 + jnp.exp(-c))
    o_ref[...] = jnp.dot(s, w_ref[...], precision=HIGHEST, preferred_element_type=F32) + b_ref[...]


def _mod_call(cc, w_mod, b_mod):
    depth, _, width = w_mod.shape
    tn = 1536
    return pl.pallas_call(
        _mod_kernel,
        out_shape=jax.ShapeDtypeStruct((depth, 8, width), F32),
        grid=(depth, width // tn),
        in_specs=[
            pl.BlockSpec((8, D), lambda l, j: (0, 0)),
            pl.BlockSpec((None, D, tn), lambda l, j: (l, 0, j)),
            pl.BlockSpec((None, 1, tn), lambda l, j: (l, 0, j)),
        ],
        out_specs=pl.BlockSpec((None, 8, tn), lambda l, j: (l, 0, j)),
        compiler_params=_cparams("arbitrary", "arbitrary"),
        name="mod",
    )(cc, w_mod, b_mod.reshape(depth, 1, width))


def _qkv_kernel(x_ref, sc_ref, sh_ref, w_ref, cos_ref, sin_ref, q_ref, k_ref, vt_ref):
    h = x_ref[...] * (1.0 + sc_ref[...]) + sh_ref[...]
    y = jnp.dot(h.astype(BF16), w_ref[...], preferred_element_type=F32)
    cos = cos_ref[...]
    sin = sin_ref[...]
    lane = lax.broadcasted_iota(jnp.int32, cos.shape, 1)
    first_half = (lane % (2 * ROPE_FREQS)) < ROPE_FREQS

    def rope(t):
        partner = jnp.where(first_half,
                            pltpu.roll(t, LANES - ROPE_FREQS, axis=1),
                            pltpu.roll(t, ROPE_FREQS, axis=1))
        return t * cos + partner * sin

    for j in range(D // LANES):
        cols = slice(j * LANES, (j + 1) * LANES)
        q_ref[:, cols] = (rope(y[:, cols]) * Q_SCALE).astype(BF16)
        k_ref[:, cols] = rope(y[:, D + j * LANES:D + (j + 1) * LANES]).astype(BF16)
    vt_ref[...] = y[:, 2 * D:].T.astype(BF16)


def _qkv_call(xs, sc, sh, w, cos, sin, n_lat, n_ctx, batch):
    t = xs.shape[0]
    tm = ROW_TILE
    lat_tiles = n_lat // tm
    all_lat = batch * lat_tiles

    def mod_idx(i):
        return (jnp.minimum(i // lat_tiles, batch), 0, 0)

    def rope_idx(i):
        return (jnp.where(i < all_lat, i % lat_tiles, lat_tiles), 0)

    def kv_idx(i):
        return (jnp.where(i < all_lat, i // lat_tiles, i - all_lat),
                jnp.where(i < all_lat, i % lat_tiles, lat_tiles), 0)

    def vt_idx(i):
        return kv_idx(i)[:2] + (0, 0)

    return pl.pallas_call(
        _qkv_kernel,
        out_shape=(jax.ShapeDtypeStruct((t, D), BF16),
                   jax.ShapeDtypeStruct((batch, n_lat + n_ctx, D), BF16),
                   jax.ShapeDtypeStruct((batch, lat_tiles + 1, D, tm), BF16)),
        grid=(t // tm,),
        in_specs=[
            pl.BlockSpec((tm, D), lambda i: (i, 0)),
            pl.BlockSpec((None, 1, D), mod_idx),
            pl.BlockSpec((None, 1, D), mod_idx),
            pl.BlockSpec((D, 3 * D), lambda i: (0, 0)),
            pl.BlockSpec((tm, LANES), rope_idx),
            pl.BlockSpec((tm, LANES), rope_idx),
        ],
        out_specs=(
            pl.BlockSpec((tm, D), lambda i: (i, 0)),
            pl.BlockSpec((None, tm, D), kv_idx),
            pl.BlockSpec((None, None, D, tm), vt_idx),
        ),
        compiler_params=_cparams("arbitrary"),
        name="qkv_rope",
    )(xs, sc, sh, w, cos, sin)


def _rope_tables(n_lat, pad_rows):
    rows = n_lat // GRID_W
    row_ids = jnp.repeat(jnp.arange(rows), GRID_W).astype(F32)
    col_ids = jnp.tile(jnp.arange(GRID_W), rows).astype(F32)
    inv_freq = ROPE_THETA ** (-jnp.arange(ROPE_FREQS, dtype=F32) / ROPE_FREQS)
    ar = row_ids[:, None] * inv_freq
    ac = col_ids[:, None] * inv_freq
    cos = jnp.concatenate([jnp.cos(ar), jnp.cos(ar), jnp.cos(ac), jnp.cos(ac)], axis=1)
    sin = jnp.concatenate([-jnp.sin(ar), jnp.sin(ar), -jnp.sin(ac), jnp.sin(ac)], axis=1)
    reps = LANES // HEAD_DIM
    cos = jnp.tile(cos, (1, reps))
    sin = jnp.tile(sin, (1, reps))
    cos = jnp.concatenate([cos, jnp.ones((pad_rows, LANES), F32)], axis=0)
    sin = jnp.concatenate([sin, jnp.zeros((pad_rows, LANES), F32)], axis=0)
    return cos, sin


def _attn_kernel(lam_ref, g_ref, q_ref, k_ref, vt_ref, o_ref, qq_ref, s_ref, m_ref, acc_ref, *,
                 n_chunks, chunks_per_step, lambda_init):
    tq = q_ref.shape[0]
    kc = vt_ref.shape[2]
    qt = q_ref[...].astype(F32).T
    sub = lax.broadcasted_iota(jnp.int32, qt.shape, 0)
    qq_ref[...] = jnp.concatenate([jnp.where(sub < HEAD_DIM, qt, 0.0),
                                   jnp.where(sub >= HEAD_DIM, qt, 0.0)], axis=1).astype(BF16)
    m_ref[...] = jnp.full(m_ref.shape, -jnp.inf, F32)
    acc_ref[...] = jnp.zeros(acc_ref.shape, F32)
    ones_rows = jnp.ones((SUM_ROWS, kc), BF16)

    def scores(slot, c0, nc):
        start = c0 * kc
        if not isinstance(start, int):
            start = pl.multiple_of(start, kc)
        kj = k_ref[pl.ds(start, nc * kc), :]
        s_ref[slot, 0:nc * kc, :] = jnp.dot(kj, qq_ref[...], preferred_element_type=F32)

    def accumulate(slot, c0, nc):
        s = s_ref[slot, 0:nc * kc, :]
        m_old = m_ref[...]
        m_new = jnp.maximum(m_old, jnp.max(s, axis=0, keepdims=True))
        alpha = jnp.exp2(m_old - m_new)
        pb = jnp.exp2(s - m_new).astype(BF16)
        pv = None
        for c in range(nc):
            v_ext = jnp.concatenate([vt_ref[c0 + c], ones_rows], axis=0)
            part = jnp.dot(v_ext, pb[c * kc:(c + 1) * kc], preferred_element_type=F32)
            pv = part if pv is None else pv + part
        acc_ref[...] = alpha * acc_ref[...] + pv
        m_ref[...] = m_new

    cps = chunks_per_step
    n_full = n_chunks // cps
    steps = [(i * cps, cps) for i in range(n_full)]
    if n_chunks % cps:
        steps.append((n_full * cps, n_chunks % cps))
    n_pairs = max(n_full - 1, 0) // 2
    scores(0, *steps[0])
    if n_pairs > 0:
        def body(jj, carry):
            c0 = jj * (2 * cps)
            scores(1, c0 + cps, cps)
            accumulate(0, c0, cps)
            scores(0, c0 + 2 * cps, cps)
            accumulate(1, c0 + cps, cps)
            return carry
        lax.fori_loop(0, n_pairs, body, 0)
    slot = 0
    for idx in range(2 * n_pairs, len(steps)):
        if idx + 1 < len(steps):
            scores(1 - slot, *steps[idx + 1])
        accumulate(slot, *steps[idx])
        slot = 1 - slot

    lam = lam_ref[...]
    lam_val = (jnp.exp(jnp.sum(lam[0:1] * lam[1:2], axis=1, keepdims=True))
               - jnp.exp(jnp.sum(lam[2:3] * lam[3:4], axis=1, keepdims=True)) + lambda_init)
    o_all = acc_ref[0:V_DIM, :] / acc_ref[V_DIM:V_DIM + 1, :]
    o = (o_all[:, :tq] - lam_val * o_all[:, tq:]).T
    o = o * lax.rsqrt(jnp.mean(o * o, axis=-1, keepdims=True) + RMS_EPS)
    o_ref[...] = (o * g_ref[...] * (1.0 - lambda_init)).astype(o_ref.dtype)


def _attn_call(q, k_all, vt_all, lam, subln_g, lambda_init, *, batch, q_rows, q_row0, tq,
               chunk0, n_chunks):
    nq = q_rows // tq
    q_blk0 = q_row0 // tq
    kc = vt_all.shape[3]
    key_blk = chunk0 // n_chunks
    kern = functools.partial(_attn_kernel, n_chunks=n_chunks,
                             chunks_per_step=ATTN_K_TILE // kc, lambda_init=lambda_init)
    return pl.pallas_call(
        kern,
        out_shape=jax.ShapeDtypeStruct((batch * q_rows, D), BF16),
        grid=(batch, HEADS, nq),
        in_specs=[
            pl.BlockSpec((4, HEAD_DIM), lambda b, h, i: (0, 0)),
            pl.BlockSpec((1, V_DIM), lambda b, h, i: (0, 0)),
            pl.BlockSpec((tq, V_DIM), lambda b, h, i: (q_blk0 + b * nq + i, h)),
            pl.BlockSpec((None, n_chunks * kc, V_DIM), lambda b, h, i: (b, key_blk, h)),
            pl.BlockSpec((None, n_chunks, V_DIM, kc), lambda b, h, i: (b, key_blk, h, 0)),
        ],
        out_specs=pl.BlockSpec((tq, V_DIM), lambda b, h, i: (b * nq + i, h)),
        scratch_shapes=[
            pltpu.VMEM((V_DIM, 2 * tq), BF16),
            pltpu.VMEM((2, ATTN_K_TILE, 2 * tq), F32),
            pltpu.VMEM((1, 2 * tq), F32),
            pltpu.VMEM((V_DIM + SUM_ROWS, 2 * tq), F32),
        ],
        compiler_params=_cparams("arbitrary", "arbitrary", "arbitrary"),
        name="diff_attn",
    )(lam, subln_g.reshape(1, V_DIM), q, k_all, vt_all)


def _proj_ln_kernel(a_ref, w_ref, x_ref, gate_ref, lng_ref, lnb_ref, o_ref):
    y = jnp.dot(a_ref[...], w_ref[...], preferred_element_type=F32)
    z = DEEPNORM_ALPHA * x_ref[...] + gate_ref[...] * y
    o_ref[...] = _layer_norm_rows(z, lng_ref[...], lnb_ref[...])


def _proj_ln_call(a, w, xs, gate, ln_g, ln_b, mod_idx):
    t = xs.shape[0]
    tm = ROW_TILE
    return pl.pallas_call(
        _proj_ln_kernel,
        out_shape=jax.ShapeDtypeStruct((t, D), F32),
        grid=(t // tm,),
        in_specs=[
            pl.BlockSpec((tm, D), lambda i: (i, 0)),
            pl.BlockSpec((D, D), lambda i: (0, 0)),
            pl.BlockSpec((tm, D), lambda i: (i, 0)),
            pl.BlockSpec((None, 1, D), mod_idx),
            pl.BlockSpec((1, D), lambda i: (0, 0)),
            pl.BlockSpec((1, D), lambda i: (0, 0)),
        ],
        out_specs=pl.BlockSpec((tm, D), lambda i: (i, 0)),
        compiler_params=_cparams("arbitrary"),
        name="proj_ln",
    )(a, w, xs, gate, ln_g.reshape(1, D), ln_b.reshape(1, D))


def _fn_in_kernel(x_ref, sc_ref, sh_ref, w_ref, g_ref, cs_ref, a_ref, b_ref, *scratch, radix):
    h = x_ref[...] * (1.0 + sc_ref[...]) + sh_ref[...]
    u = jnp.dot(h.astype(BF16), w_ref[...], preferred_element_type=F32)
    gain = g_ref[...]
    cs = cs_ref[...]
    for g in range(FN_GROUPS):
        cols = slice(g * FN_GROUP_DIM, (g + 1) * FN_GROUP_DIM)
        ug = u[:, cols]
        r = ug * lax.rsqrt(jnp.mean(ug * ug, axis=-1, keepdims=True) + RMS_EPS) * gain[:, cols]
        ab = jnp.dot(r.astype(BF16), cs, preferred_element_type=F32)
        if radix:
            scratch[0][g] = ab[:, :FN_GROUP_DIM]
            scratch[1][g] = ab[:, FN_GROUP_DIM:]
        else:
            a_ref[:, cols] = ab[:, :FN_GROUP_DIM].astype(BF16)
            b_ref[:, cols] = ab[:, FN_GROUP_DIM:].astype(BF16)
    if radix:
        rows = x_ref.shape[0] // radix
        for q in range(radix):
            for out_ref, scr in ((a_ref, scratch[0]), (b_ref, scratch[1])):
                out_ref[q] = jnp.concatenate(
                    [scr[g, pl.ds(q, rows, stride=radix), :] for g in range(FN_GROUPS)],
                    axis=1).astype(BF16)


def _fn_in_call(xs, sc, sh, w, norm_g, cs, *, row0, n_rows, seq, radix, mod_per_seq):
    tm = ROW_TILE
    blk0 = row0 // tm
    seq_tiles = seq // tm
    last_mod = sc.shape[0] - 1

    def mod_idx(i):
        return (i // seq_tiles if mod_per_seq else last_mod, 0, 0)

    if radix:
        out = jax.ShapeDtypeStruct((n_rows // seq, radix, seq // radix, D), BF16)
        out_spec = pl.BlockSpec((None, radix, tm // radix, D),
                                lambda i: (i // seq_tiles, 0, i % seq_tiles, 0))
        scratch = [pltpu.VMEM((FN_GROUPS, tm, FN_GROUP_DIM), F32),
                   pltpu.VMEM((FN_GROUPS, tm, FN_GROUP_DIM), F32)]
    else:
        out = jax.ShapeDtypeStruct((n_rows, D), BF16)
        out_spec = pl.BlockSpec((tm, D), lambda i: (i, 0))
        scratch = []
    return pl.pallas_call(
        functools.partial(_fn_in_kernel, radix=radix),
        out_shape=(out, out),
        grid=(n_rows // tm,),
        in_specs=[
            pl.BlockSpec((tm, D), lambda i: (blk0 + i, 0)),
            pl.BlockSpec((None, 1, D), mod_idx),
            pl.BlockSpec((None, 1, D), mod_idx),
            pl.BlockSpec((D, D), lambda i: (0, 0)),
            pl.BlockSpec((1, D), lambda i: (0, 0)),
            pl.BlockSpec((FN_GROUP_DIM, 2 * FN_GROUP_DIM), lambda i: (0, 0)),
        ],
        out_specs=(out_spec, out_spec),
        scratch_shapes=scratch,
        compiler_params=_cparams("arbitrary"),
        name="fn_in",
    )(xs, sc, sh, w, norm_g.reshape(1, D), cs)


def _fft_kernel(m_ref, cw_ref, sw_ref, a_ref, b_ref, o_ref, acc_ref, *, scale):
    radix, p, tc = a_ref.shape
    reps = tc // LANES
    cw1 = cw_ref[...]
    sw1 = sw_ref[...]
    cwa, swa = cw1, sw1
    for a in range(radix):
        v = jnp.concatenate([a_ref[a], b_ref[a]], axis=0)
        g = jnp.dot(m_ref[...], v, preferred_element_type=F32)
        gr, hi = g[:p], g[p:]
        if a == 0:
            tre, uim = gr, hi
        else:
            cwt = jnp.tile(cwa, (1, reps))
            swt = jnp.tile(swa, (1, reps))
            tre = gr * cwt - hi * swt
            uim = gr * swt + hi * cwt
            cwa, swa = cwa * cw1 - swa * sw1, swa * cw1 + cwa * sw1
        for c in range(radix):
            k8 = (8 * a * c) // radix if (8 * a * c) % radix == 0 else None
            ang = 2.0 * np.pi * a * c / radix
            co, si = float(np.cos(ang)), float(np.sin(ang))
            if k8 is not None and k8 % 2 == 0:
                co, si = [(1.0, 0.0), (0.0, 1.0), (-1.0, 0.0), (0.0, -1.0)][(k8 // 2) % 4]
            term = None
            if co != 0.0:
                term = tre if co == 1.0 else (-tre if co == -1.0 else co * tre)
            if si != 0.0:
                part = uim if si == 1.0 else (-uim if si == -1.0 else si * uim)
                term = -part if term is None else term - part
            if a == 0:
                acc_ref[c] = term
            else:
                acc_ref[c] += term
    o_ref[...] = (acc_ref[...] * scale).astype(o_ref.dtype)


def _fft_call(a, bp, radix):
    n_seq, _, p, _ = a.shape
    seq = radix * p
    tc = 2 * LANES
    cp, sp = _dft_tables(p)
    m = jnp.concatenate([jnp.concatenate([cp, -sp], axis=1),
                         jnp.concatenate([sp, cp], axis=1)], axis=0).astype(BF16)
    ang = jnp.arange(p, dtype=F32) * (2.0 * np.pi / seq)
    cw = jnp.broadcast_to(jnp.cos(ang)[:, None], (p, LANES))
    sw = jnp.broadcast_to(jnp.sin(ang)[:, None], (p, LANES))
    kern = functools.partial(_fft_kernel, scale=float((seq * FN_GROUP_DIM) ** -0.5))
    seq_spec = pl.BlockSpec((None, radix, p, tc), lambda b, j: (b, 0, 0, j))
    return pl.pallas_call(
        kern,
        out_shape=jax.ShapeDtypeStruct(a.shape, BF16),
        grid=(n_seq, D // tc),
        in_specs=[
            pl.BlockSpec((2 * p, 2 * p), lambda b, j: (0, 0)),
            pl.BlockSpec((p, LANES), lambda b, j: (0, 0)),
            pl.BlockSpec((p, LANES), lambda b, j: (0, 0)),
            seq_spec, seq_spec,
        ],
        out_specs=seq_spec,
        scratch_shapes=[pltpu.VMEM((radix, p, tc), F32)],
        compiler_params=_cparams("arbitrary", "arbitrary"),
        name="seq_fft",
    )(m, cw, sw, a, bp)


def _dft_tables(n):
    idx = jnp.arange(n, dtype=jnp.int32)
    prod = (idx[:, None] * idx[None, :]) % n
    ang = prod.astype(F32) * (2.0 * np.pi / n)
    return jnp.cos(ang), jnp.sin(ang)


def _dft_kernel(c_ref, s_ref, a0_ref, a1_ref, b0_ref, b1_ref, o_ref, acc_ref, *, scale):
    kj = pl.program_id(1)

    @pl.when(kj == 0)
    def _():
        acc_ref[...] = jnp.zeros(acc_ref.shape, F32)

    c = c_ref[...]
    s = s_ref[...]
    acc_ref[0] += (jnp.dot(c, a0_ref[...], preferred_element_type=F32)
                   - jnp.dot(s, b0_ref[...], preferred_element_type=F32))
    acc_ref[1] += (jnp.dot(c, a1_ref[...], preferred_element_type=F32)
                   - jnp.dot(s, b1_ref[...], preferred_element_type=F32))

    @pl.when(kj == pl.num_programs(1) - 1)
    def _():
        o_ref[...] = (acc_ref[...] * scale).astype(o_ref.dtype)


def _dft_call(ct, st, a, bp, *, seq, row0, tile):
    nt = seq // tile
    blk0 = row0 // tile
    kern = functools.partial(_dft_kernel, scale=float((seq * FN_GROUP_DIM) ** -0.5))
    seq_spec0 = pl.BlockSpec((tile, D), lambda i, j: (blk0 + j, 0))
    seq_spec1 = pl.BlockSpec((tile, D), lambda i, j: (blk0 + nt + j, 0))
    return pl.pallas_call(
        kern,
        out_shape=jax.ShapeDtypeStruct((2, seq, D), BF16),
        grid=(nt, nt),
        in_specs=[
            pl.BlockSpec((tile, tile), lambda i, j: (i, j)),
            pl.BlockSpec((tile, tile), lambda i, j: (i, j)),
            seq_spec0, seq_spec1, seq_spec0, seq_spec1,
        ],
        out_specs=pl.BlockSpec((2, tile, D), lambda i, j: (0, i, 0)),
        scratch_shapes=[pltpu.VMEM((2, tile, D), F32)],
        compiler_params=_cparams("arbitrary", "arbitrary"),
        name="seq_dft",
    )(ct, st, a, a, bp, bp)


def _router_kernel(x_ref, sc_ref, sh_ref, w_ref, b_ref, h_ref, meta_ref, gate_ref, cnt_ref,
                   carry_ref):
    i = pl.program_id(0)

    @pl.when(i == 0)
    def _():
        carry_ref[...] = jnp.zeros(carry_ref.shape, F32)

    h = x_ref[...] * (1.0 + sc_ref[...]) + sh_ref[...]
    _store_token_tiles(h_ref, h)
    logits = jnp.dot(h, w_ref[...], precision=HIGHEST, preferred_element_type=F32) + b_ref[...]
    tm = logits.shape[0]
    lane = lax.broadcasted_iota(jnp.int32, logits.shape, 1).astype(F32)
    wide = lax.broadcasted_iota(jnp.int32, (tm, LANES), 1)

    work = logits
    onehot = jnp.zeros(logits.shape, F32)
    picks, vals = [], []
    for _ in range(TOP_K):
        mx = jnp.max(work, axis=1, keepdims=True)
        idx = jnp.min(jnp.where(work == mx, lane, float(N_EXPERTS)), axis=1, keepdims=True)
        sel = lane == idx
        onehot = onehot + sel.astype(F32)
        work = jnp.where(sel, -jnp.inf, work)
        picks.append(idx)
        vals.append(mx)

    exps = [jnp.exp(v - vals[0]) for v in vals]
    denom = exps[0] + exps[1] + exps[2] + exps[3]

    r_i = lax.broadcasted_iota(jnp.int32, (tm, tm), 0)
    c_i = lax.broadcasted_iota(jnp.int32, (tm, tm), 1)
    tri = (c_i < r_i).astype(BF16)
    prefix = jnp.dot(tri, onehot.astype(BF16), preferred_element_type=F32) + carry_ref[0:1, 0:N_EXPERTS]

    meta = jnp.zeros((tm, LANES), jnp.int32)
    gates = jnp.zeros((tm, LANES), F32)
    for k in range(TOP_K):
        rank = jnp.sum(jnp.where(lane == picks[k], prefix, 0.0), axis=1, keepdims=True)
        meta = jnp.where(wide == k, picks[k].astype(jnp.int32), meta)
        meta = jnp.where(wide == TOP_K + k, rank.astype(jnp.int32), meta)
        gates = jnp.where(wide == k, exps[k] / denom, gates)
    meta_ref[...] = meta
    gate_ref[...] = gates

    total = carry_ref[0:1, 0:N_EXPERTS] + jnp.sum(onehot, axis=0, keepdims=True)
    carry_ref[0:1, 0:N_EXPERTS] = total
    cnt_ref[...] = jnp.zeros(cnt_ref.shape, jnp.int32)
    cnt_ref[0:1, 0:N_EXPERTS] = total.astype(jnp.int32)


def _router_call(xs, sc, sh, w, b, mod_idx):
    t = xs.shape[0]
    tm = ROW_TILE
    return pl.pallas_call(
        _router_kernel,
        out_shape=(jax.ShapeDtypeStruct((t * TOKEN_TILE_ROWS, LANES), F32),
                   jax.ShapeDtypeStruct((t, LANES), jnp.int32),
                   jax.ShapeDtypeStruct((t, LANES), F32),
                   jax.ShapeDtypeStruct((8, LANES), jnp.int32)),
        grid=(t // tm,),
        in_specs=[
            pl.BlockSpec((tm, D), lambda i: (i, 0)),
            pl.BlockSpec((None, 1, D), mod_idx),
            pl.BlockSpec((None, 1, D), mod_idx),
            pl.BlockSpec((D, N_EXPERTS), lambda i: (0, 0)),
            pl.BlockSpec((1, N_EXPERTS), lambda i: (0, 0)),
        ],
        out_specs=(pl.BlockSpec((tm * TOKEN_TILE_ROWS, LANES), lambda i: (i, 0)),
                   pl.BlockSpec((tm, LANES), lambda i: (i, 0)),
                   pl.BlockSpec((tm, LANES), lambda i: (i, 0)),
                   pl.BlockSpec((8, LANES), lambda i: (0, 0))),
        scratch_shapes=[pltpu.VMEM((8, LANES), F32)],
        compiler_params=_cparams("arbitrary"),
        name="router",
    )(xs, sc, sh, w, b.reshape(1, N_EXPERTS))


def _expert_kernel(blk_e_ref, n_used_ref, words_ref, h_ref, wgu_ref, bgu_ref, wd_ref, bd_ref,
                   y_ref, wgu_bf, wd_bf, xbuf0, xbuf1, ybuf0, ybuf1, gsem, ssem, *, trash_row0):
    i = pl.program_id(0)
    tr = TOKEN_TILE_ROWS
    blk = xbuf0.shape[0] // tr
    f = wd_ref.shape[0]
    n_used = n_used_ref[0]
    xbufs = (xbuf0, xbuf1)
    ybufs = (ybuf0, ybuf1)

    def word(j, r):
        return words_ref[(j + 1) * blk + r]

    def tile_rows(r):
        if not isinstance(r, int):
            return pl.ds(pl.multiple_of(r * tr, tr), tr)
        return pl.ds(r * tr, tr)

    def gather_copy(tok, r, dst):
        return pltpu.make_async_copy(h_ref.at[tile_rows(tok)], xbufs[dst].at[tile_rows(r)],
                                     gsem.at[dst])

    def scatter_copy(row, r, src):
        return pltpu.make_async_copy(ybufs[src].at[tile_rows(r)], y_ref.at[tile_rows(row)],
                                     ssem.at[src])

    def start_gather(j, r, dst):
        prio = r % 2 if isinstance(r, int) else 0
        gather_copy(lax.shift_right_logical(word(j, r), SLOT_ROW_BITS), r, dst).start(priority=prio)

    def start_scatter(j, r, src):
        prio = r % 2 if isinstance(r, int) else 0
        scatter_copy(word(j, r) & SLOT_ROW_MASK, r, src).start(priority=prio)

    def wait_gathers(dst):
        _loop_unrolled(blk, DMA_WAIT_UNROLL, lambda r: gather_copy(0, 0, dst).wait())

    def wait_scatters(src):
        _loop_unrolled(blk, DMA_WAIT_UNROLL, lambda r: scatter_copy(0, 0, src).wait())

    @pl.when(i == 0)
    def _():
        ybuf0[...] = jnp.zeros(ybuf0.shape, F32)
        ybuf1[...] = jnp.zeros(ybuf1.shape, F32)
        init = pltpu.make_async_copy(ybuf0, y_ref.at[pl.ds(trash_row0 * tr, blk * tr)],
                                     ssem.at[0])
        init.start()
        init.wait()
        _loop_unrolled(blk, DMA_UNROLL, lambda r: start_gather(0, r, 0))

    new_expert = jnp.logical_or(i == 0, blk_e_ref[i] != blk_e_ref[jnp.maximum(i - 1, 0)])

    @pl.when(jnp.logical_and(i < n_used, new_expert))
    def _():
        wgu_bf[...] = wgu_ref[...].astype(BF16)
        wd_bf[...] = wd_ref[...].astype(BF16)

    def block_step(slot):
        other = 1 - slot
        wait_gathers(slot)

        @pl.when(i >= 1)
        def _():
            wait_scatters(slot)

        nxt = jnp.minimum(i + 1, n_used - 1)

        def start_rows(part):
            for r in range(part * blk // 4, (part + 1) * blk // 4):
                start_gather(nxt, r, other)
                start_scatter(i - 1, r, other)

        start_rows(0)
        xb = _load_token_tiles(xbufs[slot]).astype(BF16)
        gu = jnp.dot(xb, wgu_bf[...], preferred_element_type=F32) + bgu_ref[...]
        start_rows(1)
        gate = jnp.minimum(gu[:, :f], SWIGLU_LIMIT)
        up = jnp.clip(gu[:, f:], -SWIGLU_LIMIT, SWIGLU_LIMIT)
        act = gate / (1.0 + jnp.exp(-SWIGLU_ALPHA * gate)) * (up + 1.0)
        start_rows(2)
        _store_token_tiles(ybufs[slot], jnp.dot(act.astype(BF16), wd_bf[...],
                                                preferred_element_type=F32) + bd_ref[...])
        start_rows(3)

        @pl.when(i == n_used - 1)
        def _():
            wait_gathers(other)
            wait_scatters(other)
            _loop_unrolled(blk, DMA_UNROLL, lambda r: start_scatter(i, r, slot))
            wait_scatters(slot)

    for parity in range(2):
        pl.when(jnp.logical_and(i < n_used, i % 2 == parity))(
            functools.partial(block_step, parity))


def _expert_call(blk_e, n_used, words, h, w_gu, b_gu, w_down, b_down, layer):
    t = h.shape[0] // TOKEN_TILE_ROWS
    blk = EXPERT_BLOCK
    n_blocks = words.shape[0] // blk - 1
    depth, e, _, f2 = w_gu.shape
    f = w_down.shape[2]
    kern = functools.partial(_expert_kernel, trash_row0=TOP_K * t)
    return pl.pallas_call(
        kern,
        out_shape=jax.ShapeDtypeStruct(((TOP_K * t + blk) * TOKEN_TILE_ROWS, LANES), F32),
        grid_spec=pltpu.PrefetchScalarGridSpec(
            num_scalar_prefetch=3,
            grid=(n_blocks,),
            in_specs=[
                pl.BlockSpec(memory_space=pl.ANY),
                pl.BlockSpec((None, None, D, f2), lambda i, be, nu, wo: (layer, be[i], 0, 0)),
                pl.BlockSpec((None, None, 1, f2), lambda i, be, nu, wo: (layer, be[i], 0, 0)),
                pl.BlockSpec((None, None, f, D), lambda i, be, nu, wo: (layer, be[i], 0, 0)),
                pl.BlockSpec((None, None, 1, D), lambda i, be, nu, wo: (layer, be[i], 0, 0)),
            ],
            out_specs=pl.BlockSpec(memory_space=pl.ANY),
            scratch_shapes=[
                pltpu.VMEM((D, f2), BF16), pltpu.VMEM((f, D), BF16),
                pltpu.VMEM((blk * TOKEN_TILE_ROWS, LANES), F32),
                pltpu.VMEM((blk * TOKEN_TILE_ROWS, LANES), F32),
                pltpu.VMEM((blk * TOKEN_TILE_ROWS, LANES), F32),
                pltpu.VMEM((blk * TOKEN_TILE_ROWS, LANES), F32),
                pltpu.SemaphoreType.DMA((2,)), pltpu.SemaphoreType.DMA((2,)),
            ],
        ),
        compiler_params=_cparams("arbitrary"),
        name="moe_experts",
    )(blk_e, n_used, words, h, w_gu, b_gu.reshape(depth, e, 1, f2), w_down,
      b_down.reshape(depth, e, 1, D))


def _combine_kernel(y0_ref, y1_ref, y2_ref, y3_ref, gates_ref, x_ref, g2_ref, lng_ref, lnb_ref,
                    o_ref):
    gates = gates_ref[...]
    y = gates[:, 0:1] * _load_token_tiles(y0_ref)
    for k, y_ref in enumerate((y1_ref, y2_ref, y3_ref), start=1):
        y = y + gates[:, k:k + 1] * _load_token_tiles(y_ref)
    z = DEEPNORM_ALPHA * x_ref[...] + g2_ref[...] * y
    o_ref[...] = _layer_norm_rows(z, lng_ref[...], lnb_ref[...])


def _combine_call(y, gates, xs, g2, ln_g, ln_b, mod_idx):
    t = xs.shape[0]
    tm = ROW_TILE
    nt = t // tm

    def y_spec(k):
        return pl.BlockSpec((tm * TOKEN_TILE_ROWS, LANES), lambda i: (k * nt + i, 0))

    return pl.pallas_call(
        _combine_kernel,
        out_shape=jax.ShapeDtypeStruct((t, D), F32),
        grid=(nt,),
        in_specs=[y_spec(k) for k in range(TOP_K)] + [
            pl.BlockSpec((tm, LANES), lambda i: (i, 0)),
            pl.BlockSpec((tm, D), lambda i: (i, 0)),
            pl.BlockSpec((None, 1, D), mod_idx),
            pl.BlockSpec((1, D), lambda i: (0, 0)),
            pl.BlockSpec((1, D), lambda i: (0, 0)),
        ],
        out_specs=pl.BlockSpec((tm, D), lambda i: (i, 0)),
        compiler_params=_cparams("arbitrary"),
        name="moe_combine_ln",
    )(y, y, y, y, gates, xs, g2, ln_g.reshape(1, D), ln_b.reshape(1, D))


def _moe_plan(meta, counts, cap):
    blk = EXPERT_BLOCK
    t = meta.shape[0]
    assert TOP_K * t + blk <= 1 << SLOT_ROW_BITS and t <= 1 << (32 - SLOT_ROW_BITS)
    ids = meta[:, :TOP_K]
    rank = meta[:, TOP_K:2 * TOP_K]
    cnt = counts[0, :N_EXPERTS]
    padded = (cnt + blk - 1) // blk * blk
    pend = jnp.cumsum(padded)
    pstart = pend - padded
    dest = (pstart[ids] + rank).reshape(-1).astype(jnp.int32)
    tok = jnp.arange(t, dtype=jnp.int32)[:, None]
    real = (tok << SLOT_ROW_BITS) | (jnp.arange(TOP_K, dtype=jnp.int32)[None, :] * t + tok)
    pad_tok = np.uint32((t - 1) << SLOT_ROW_BITS).astype(np.int32)
    pad = pad_tok | (TOP_K * t + jnp.arange(blk + cap, dtype=jnp.int32) % blk)
    words = pad.astype(jnp.int32).at[blk + dest].set(real.reshape(-1), unique_indices=True,
                                                     mode="promise_in_bounds")
    n_blocks = cap // blk
    starts = jnp.arange(n_blocks, dtype=pend.dtype) * blk
    blk_e = jnp.minimum(jnp.sum((pend[None, :] <= starts[:, None]).astype(jnp.int32), axis=1),
                        N_EXPERTS - 1).astype(jnp.int32)
    n_used = (pend[-1:] // blk).astype(jnp.int32)
    return words, blk_e, n_used


def kernel(x, c, ctx, c_ctx, w_mod, b_mod, ln_g, ln_b, attn_w_qkv, attn_lambda, attn_subln_g,
           attn_w_o, fn_w_in, fn_norm_g, fn_w_out, moe_w_router, moe_b_router, moe_w_gu,
           moe_b_gu, moe_w_down, moe_b_down):
    batch, n_lat, d = x.shape
    n_ctx = ctx.shape[1]
    assert d == D and batch == 2 and n_lat % (4 * ROW_TILE) == 0 and n_ctx == ROW_TILE
    t_lat = batch * n_lat
    t_all = t_lat + batch * n_ctx
    lat_tiles = n_lat // ROW_TILE
    last_ctx_layer = max(i for i in range(DEPTH) if i % N_MIXERS == 0)

    def mod_idx(i):
        return (jnp.minimum(i // lat_tiles, batch), 0, 0)

    xs = jnp.concatenate([x.reshape(t_lat, D), ctx.reshape(batch * n_ctx, D)], axis=0)
    cc = jnp.concatenate([c, c_ctx[None, :], jnp.zeros((8 - batch - 1, D), F32)], axis=0)
    mods = _mod_call(cc, w_mod, b_mod)

    cos, sin = _rope_tables(n_lat, ROW_TILE)
    cs = jnp.concatenate(_dft_tables(FN_GROUP_DIM), axis=1).astype(BF16)
    ct_ctx, st_ctx = [m.astype(BF16) for m in _dft_tables(n_ctx)]

    n_assign = t_all * TOP_K
    cap = -(-n_assign // EXPERT_BLOCK) * EXPERT_BLOCK + N_EXPERTS * EXPERT_BLOCK

    for i in range(DEPTH):
        j = i // N_MIXERS
        ctx_full = i < last_ctx_layer

        def mvec(k, i=i):
            return mods[i, :batch + 1, k * D:(k + 1) * D].reshape(batch + 1, 1, D)

        sh1, sc1, g1, sh2, sc2, g2 = [mvec(k) for k in range(6)]

        if i % N_MIXERS == 0:
            lambda_init = 0.8 - 0.6 * float(np.exp(-0.3 * i))
            q, k_all, vt_all = _qkv_call(xs, sc1, sh1, attn_w_qkv[j].astype(BF16), cos, sin,
                                         n_lat, n_ctx, batch)
            o_lat = _attn_call(q, k_all, vt_all, attn_lambda[j], attn_subln_g[j], lambda_init,
                               batch=batch, q_rows=n_lat, q_row0=0, tq=ATTN_Q_TILE,
                               chunk0=0, n_chunks=lat_tiles + 1)
            if ctx_full:
                o_ctx = _attn_call(q, k_all, vt_all, attn_lambda[j], attn_subln_g[j], lambda_init,
                                   batch=batch, q_rows=n_ctx, q_row0=t_lat, tq=n_ctx,
                                   chunk0=lat_tiles, n_chunks=1)
            else:
                o_ctx = jnp.zeros((batch * n_ctx, D), BF16)
            branch = jnp.concatenate([o_lat, o_ctx], axis=0)
            w_out = attn_w_o[j]
        else:
            w_in = fn_w_in[j].astype(BF16)
            a, bp = _fn_in_call(xs, sc1, sh1, w_in, fn_norm_g[j], cs, row0=0, n_rows=t_lat,
                                seq=n_lat, radix=FFT_RADIX, mod_per_seq=True)
            f_lat = _fft_call(a, bp, FFT_RADIX)
            if ctx_full:
                a, bp = _fn_in_call(xs, sc1, sh1, w_in, fn_norm_g[j], cs, row0=t_lat,
                                    n_rows=batch * n_ctx, seq=n_ctx, radix=0, mod_per_seq=False)
                f_ctx = _dft_call(ct_ctx, st_ctx, a, bp, seq=n_ctx, row0=0, tile=n_ctx)
            else:
                f_ctx = jnp.zeros((batch, n_ctx, D), BF16)
            branch = jnp.concatenate([f_lat.reshape(t_lat, D), f_ctx.reshape(batch * n_ctx, D)],
                                     axis=0)
            w_out = fn_w_out[j]
        xs = _proj_ln_call(branch, w_out.astype(BF16), xs, g1, ln_g[i, 0], ln_b[i, 0], mod_idx)

        h2, meta, gates, counts = _router_call(xs, sc2, sh2, moe_w_router[i], moe_b_router[i],
                                               mod_idx)
        words, blk_e, n_used = _moe_plan(meta, counts, cap)
        y = _expert_call(blk_e, n_used, words, h2, moe_w_gu, moe_b_gu, moe_w_down, moe_b_down, i)
        xs = _combine_call(y, gates, xs, g2, ln_g[i, 1], ln_b[i, 1], mod_idx)

    return xs[:t_lat].reshape(batch, n_lat, D)
```
